```python
import jax, jax.numpy as jnp
from jax import lax
import numpy as np

D_MODEL = 1024
BATCH = 8
SEQ = 2048
DEPTH = 2
DEC_BATCH = 128
DEC_SEQ = 8
PAST_LEN = 8192
PAGE_SIZE = 128

N_MIXERS = 2
N_A_LAYERS = (DEPTH + 1) // 2
N_B_LAYERS = DEPTH // 2
CHUNK = 128
SGU_WIDTH = 2 * D_MODEL
SGU_GROUPS = 8
SGU_GROUP_DIM = SGU_WIDTH // SGU_GROUPS
HEAD_DIM = 64
N_HEADS = D_MODEL // HEAD_DIM
N_KV_HEADS = 4
GQA_GROUP = N_HEADS // N_KV_HEADS
WINDOW = 128
ROPE_THETA = 10000.0
D_FF = 2816
CONV_W = 3
EPS = 1e-6

kernel_name = 'hybrid_sgu_swa_convffn_step'


def rms_norm(x, g):
    xf = x.astype(jnp.float32)
    y = xf * lax.rsqrt(jnp.mean(xf * xf, axis=-1, keepdims=True) + EPS)
    return (y * g.astype(jnp.float32)).astype(x.dtype)


def layer_norm(x, g, b):
    xf = x.astype(jnp.float32)
    mu = jnp.mean(xf, axis=-1, keepdims=True)
    var = jnp.mean(jnp.square(xf - mu), axis=-1, keepdims=True)
    y = (xf - mu) * lax.rsqrt(var + EPS)
    return (y * g.astype(jnp.float32) + b.astype(jnp.float32)).astype(x.dtype)


def rope(x, pos):
    inv_freq = jnp.power(jnp.float32(ROPE_THETA), -jnp.arange(0, HEAD_DIM, 2, dtype=jnp.float32) / HEAD_DIM)
    ang = pos.astype(jnp.float32)[:, None] * inv_freq[None, :]
    cos = jnp.cos(ang)[None, :, None, :]
    sin = jnp.sin(ang)[None, :, None, :]
    xf = x.astype(jnp.float32)
    x1, x2 = jnp.split(xf, 2, axis=-1)
    return jnp.concatenate([x1 * cos - x2 * sin, x2 * cos + x1 * sin], axis=-1).astype(x.dtype)


def sgu_mixer(h, w_in, ln_g, ln_b, w_s, b_s, w_out):
    bsz, length, _ = h.shape
    z = jax.nn.gelu(h @ w_in)
    u, v = jnp.split(z, 2, axis=-1)
    v = layer_norm(v, ln_g, ln_b)
    n_chunks = -(-length // CHUNK)
    pad = n_chunks * CHUNK - length
    vp = jnp.pad(v, ((0, 0), (0, pad), (0, 0))).reshape(bsz, n_chunks, CHUNK, SGU_GROUPS, SGU_GROUP_DIM)
    causal = jnp.tril(jnp.ones((CHUNK, CHUNK), dtype=bool))
    w = jnp.where(causal[None], w_s, 0.0)
    mixed = jnp.einsum('gts,bnsgc->bntgc', w, vp) + b_s.T[:, :, None]
    mixed = mixed.reshape(bsz, n_chunks * CHUNK, SGU_WIDTH)[:, :length]
    return (u * mixed) @ w_out, v


def swa_project(h, w_qkv, q_norm, k_norm, pos):
    bsz, length, _ = h.shape
    qkv = h @ w_qkv
    q, k, v = jnp.split(qkv, [N_HEADS * HEAD_DIM, (N_HEADS + N_KV_HEADS) * HEAD_DIM], axis=-1)
    q = q.reshape(bsz, length, N_HEADS, HEAD_DIM)
    k = k.reshape(bsz, length, N_KV_HEADS, HEAD_DIM)
    v = v.reshape(bsz, length, N_KV_HEADS, HEAD_DIM)
    q = rope(rms_norm(q, q_norm), pos)
    k = rope(rms_norm(k, k_norm), pos)
    return q, k, v


def sink_attention(q, k, v, q_pos, k_pos, sinks):
    scale = HEAD_DIM ** -0.5
    logits = jnp.einsum('bnqkgd,bnskd->bnkgqs', q, k, preferred_element_type=jnp.float32) * scale
    rel = q_pos[:, :, None] - k_pos[:, None, :]
    valid = (rel >= 0) & (rel < WINDOW) & (k_pos[:, None, :] >= 0)
    logits = jnp.where(valid[None, :, None, None], logits, -jnp.inf)
    sink = sinks.astype(jnp.float32).reshape(1, 1, N_KV_HEADS, GQA_GROUP, 1, 1)
    m = jnp.maximum(jnp.max(logits, axis=-1, keepdims=True), sink)
    p = jnp.exp(logits - m)
    denom = jnp.sum(p, axis=-1, keepdims=True) + jnp.exp(sink - m)
    probs = (p / denom).astype(v.dtype)
    return jnp.einsum('bnkgqs,bnskd->bnqkgd', probs, v)


def swa_prompt(h, w_qkv, q_norm, k_norm, sinks, w_o):
    bsz, length, _ = h.shape
    pos = jnp.arange(length, dtype=jnp.int32)
    q, k, v = swa_project(h, w_qkv, q_norm, k_norm, pos)
    nb = length // WINDOW
    qb = q.reshape(bsz, nb, WINDOW, N_KV_HEADS, GQA_GROUP, HEAD_DIM)
    kb = k.reshape(bsz, nb, WINDOW, N_KV_HEADS, HEAD_DIM)
    vb = v.reshape(bsz, nb, WINDOW, N_KV_HEADS, HEAD_DIM)
    shift = ((0, 0), (1, 0), (0, 0), (0, 0), (0, 0))
    k_band = jnp.concatenate([jnp.pad(kb, shift)[:, :-1], kb], axis=2)
    v_band = jnp.concatenate([jnp.pad(vb, shift)[:, :-1], vb], axis=2)
    q_pos = pos.reshape(nb, WINDOW)
    k_pos = jnp.concatenate([q_pos - WINDOW, q_pos], axis=1)
    o = sink_attention(qb, k_band, v_band, q_pos, k_pos, sinks)
    y = o.reshape(bsz, length, N_HEADS * HEAD_DIM) @ w_o
    return y, k[:, -WINDOW:], v[:, -WINDOW:]


def swa_sample(h, buf_k, buf_v, w_qkv, q_norm, k_norm, sinks, w_o):
    bsz, length, _ = h.shape
    buf_len = buf_k.shape[1]
    pos = PAST_LEN + jnp.arange(length, dtype=jnp.int32)
    q, k, v = swa_project(h, w_qkv, q_norm, k_norm, pos)
    k_all = jnp.concatenate([buf_k, k], axis=1)
    v_all = jnp.concatenate([buf_v, v], axis=1)
    k_pos = jnp.concatenate([PAST_LEN - buf_len + jnp.arange(buf_len, dtype=jnp.int32), pos])
    qb = q.reshape(bsz, 1, length, N_KV_HEADS, GQA_GROUP, HEAD_DIM)
    o = sink_attention(qb, k_all[:, None], v_all[:, None], pos[None], k_pos[None], sinks)
    y = o.reshape(bsz, length, N_HEADS * HEAD_DIM) @ w_o
    return y, k_all[:, -WINDOW:], v_all[:, -WINDOW:]


def conv_ffn(h, past, norm_g, w_up, conv_w, conv_b, w_down):
    length = h.shape[1]
    a = rms_norm(h, norm_g) @ w_up
    a_ext = jnp.concatenate([past.astype(a.dtype), a], axis=1)
    c = conv_b + sum(conv_w[j] * a_ext[:, j:j + length] for j in range(CONV_W))
    gate, val = jnp.split(c, 2, axis=-1)
    y = (jax.nn.silu(gate) * val) @ w_down
    return y, a_ext[:, -(CONV_W - 1):]


def setup_inputs(seed: int = 0) -> dict:
    key = jax.random.key(seed)
    ks = jax.random.split(key, 32)
    f32 = jnp.float32
    nrm = lambda k, shape, scale: jax.random.normal(k, shape, f32) * scale
    qkv_out = (N_HEADS + 2 * N_KV_HEADS) * HEAD_DIM
    buf_len = min(WINDOW, PAST_LEN)
    return {
        'x_prompt': nrm(ks[0], (BATCH, SEQ, D_MODEL), 1.0),
        'x_sample': nrm(ks[1], (DEC_BATCH, DEC_SEQ, D_MODEL), 1.0),
        'cache_swa_k': nrm(ks[2], (N_B_LAYERS, DEC_BATCH, buf_len, N_KV_HEADS, HEAD_DIM), 1.0),
        'cache_swa_v': nrm(ks[3], (N_B_LAYERS, DEC_BATCH, buf_len, N_KV_HEADS, HEAD_DIM), 1.0),
        'state_ffn_conv': nrm(ks[4], (DEPTH, DEC_BATCH, CONV_W - 1, 2 * D_FF), 1.0),
        'mix_norm_g': 1.0 + nrm(ks[5], (DEPTH, D_MODEL), 0.02),
        'sgu_w_in': nrm(ks[6], (N_A_LAYERS, D_MODEL, 2 * SGU_WIDTH), D_MODEL ** -0.5),
        'sgu_ln_g': 1.0 + nrm(ks[7], (N_A_LAYERS, SGU_WIDTH), 0.02),
        'sgu_ln_b': nrm(ks[8], (N_A_LAYERS, SGU_WIDTH), 0.02),
        'sgu_w_s': nrm(ks[9], (N_A_LAYERS, SGU_GROUPS, CHUNK, CHUNK), CHUNK ** -0.5),
        'sgu_b_s': 1.0 + nrm(ks[10], (N_A_LAYERS, SGU_GROUPS, CHUNK), 0.1),
        'sgu_w_out': nrm(ks[11], (N_A_LAYERS, SGU_WIDTH, D_MODEL), SGU_WIDTH ** -0.5),
        'attn_w_qkv': nrm(ks[12], (N_B_LAYERS, D_MODEL, qkv_out), D_MODEL ** -0.5),
        'attn_q_norm': 1.0 + nrm(ks[13], (N_B_LAYERS, HEAD_DIM), 0.02),
        'attn_k_norm': 1.0 + nrm(ks[14], (N_B_LAYERS, HEAD_DIM), 0.02),
        'attn_sinks': nrm(ks[15], (N_B_LAYERS, N_HEADS), 1.0),
        'attn_w_o': nrm(ks[16], (N_B_LAYERS, N_HEADS * HEAD_DIM, D_MODEL), (N_HEADS * HEAD_DIM) ** -0.5),
        'ffn_norm_g': 1.0 + nrm(ks[17], (DEPTH, D_MODEL), 0.02),
        'ffn_w_up': nrm(ks[18], (DEPTH, D_MODEL, 2 * D_FF), D_MODEL ** -0.5),
        'ffn_conv_w': nrm(ks[19], (DEPTH, CONV_W, 2 * D_FF), CONV_W ** -0.5),
        'ffn_conv_b': nrm(ks[20], (DEPTH, 2 * D_FF), 0.02),
        'ffn_w_down': nrm(ks[21], (DEPTH, D_FF, D_MODEL), D_FF ** -0.5),
    }


def reference(x_prompt, x_sample, cache_swa_k, cache_swa_v, state_ffn_conv,
              mix_norm_g, sgu_w_in, sgu_ln_g, sgu_ln_b, sgu_w_s, sgu_b_s, sgu_w_out,
              attn_w_qkv, attn_q_norm, attn_k_norm, attn_sinks, attn_w_o,
              ffn_norm_g, ffn_w_up, ffn_conv_w, ffn_conv_b, ffn_w_down):
    yp, ys = x_prompt, x_sample
    sgu_v_s = []
    kp, vp, ksm, vsm = [], [], [], []
    conv_p, conv_s = [], []
    for layer in range(DEPTH):
        j = layer // N_MIXERS
        hp = rms_norm(yp, mix_norm_g[layer])
        hs = rms_norm(ys, mix_norm_g[layer])
        if layer % N_MIXERS == 0:
            sgu_args = (sgu_w_in[j], sgu_ln_g[j], sgu_ln_b[j], sgu_w_s[j], sgu_b_s[j], sgu_w_out[j])
            op, _ = sgu_mixer(hp, *sgu_args)
            os_, v_rows = sgu_mixer(hs, *sgu_args)
            sgu_v_s.append(v_rows)
        else:
            op, k_p, v_p = swa_prompt(hp, attn_w_qkv[j], attn_q_norm[j], attn_k_norm[j], attn_sinks[j], attn_w_o[j])
            os_, k_s, v_s = swa_sample(hs, cache_swa_k[j], cache_swa_v[j], attn_w_qkv[j], attn_q_norm[j],
                                       attn_k_norm[j], attn_sinks[j], attn_w_o[j])
            kp.append(k_p); vp.append(v_p); ksm.append(k_s); vsm.append(v_s)
        yp = yp + op
        ys = ys + os_
        zero_past = jnp.zeros((yp.shape[0], CONV_W - 1, 2 * D_FF), dtype=yp.dtype)
        fp, cp = conv_ffn(yp, zero_past, ffn_norm_g[layer], ffn_w_up[layer], ffn_conv_w[layer], ffn_conv_b[layer], ffn_w_down[layer])
        fs, cs = conv_ffn(ys, state_ffn_conv[layer], ffn_norm_g[layer], ffn_w_up[layer], ffn_conv_w[layer], ffn_conv_b[layer], ffn_w_down[layer])
        yp = yp + fp
        ys = ys + fs
        conv_p.append(cp); conv_s.append(cs)
    return (yp, ys, jnp.stack(sgu_v_s), jnp.stack(kp), jnp.stack(vp), jnp.stack(ksm), jnp.stack(vsm),
            jnp.stack(conv_p), jnp.stack(conv_s))
```

```python
import functools

import jax
import jax.numpy as jnp
from jax import lax
from jax.experimental import pallas as pl
from jax.experimental.pallas import tpu as pltpu

F32 = jnp.float32
BF16 = jnp.bfloat16

D_MODEL = 1024
BATCH = 8
SEQ = 2048
DEC_BATCH = 128
DEC_SEQ = 8
PAST_LEN = 8192
CHUNK = 128
SGU_WIDTH = 2 * D_MODEL
SGU_GROUPS = 8
SGU_GROUP_DIM = SGU_WIDTH // SGU_GROUPS
HEAD_DIM = 64
N_HEADS = D_MODEL // HEAD_DIM
N_KV_HEADS = 4
KV_WIDTH = N_KV_HEADS * HEAD_DIM
QKV_WIDTH = D_MODEL + 2 * KV_WIDTH
WINDOW = 128
ROPE_THETA = 10000.0
D_FF = 2816
CONV_W = 3
EPS = 1e-6

LANES = 128
SUBLANES = 8
FF_CHUNK = 256
N_FF_CHUNKS = D_FF // FF_CHUNK
VMEM_LIMIT = 56 * 1024 * 1024

SGU_TM = 256
FFN_TM = 512
SWA_TQ = 512
SWA_TB = 16


def _const_spec(shape):
    nd = len(shape)
    return pl.BlockSpec(shape, lambda *_: (0,) * nd, pipeline_mode=pl.Buffered(1))


def _rms(x, g):
    ms = jnp.mean(x * x, axis=-1, keepdims=True)
    return x * lax.rsqrt(ms + EPS) * g


def _sgu_kernel(x_ref, g_ref, win_ref, lng_ref, lnb_ref, mix_ref, bias_ref, wout_ref,
                *rest, tm, emit_v):
    if emit_v:
        y_ref, v_ref, u_sc, v_sc, vb_sc, p_sc = rest
    else:
        y_ref, u_sc, v_sc, vb_sc, p_sc = rest
        v_ref = None
    x = x_ref[...]
    h = _rms(x, g_ref[...]).astype(BF16)
    cw = 512
    for c in range(SGU_WIDTH // cw):
        u_sc[:, c * cw:(c + 1) * cw] = jax.nn.gelu(
            jnp.dot(h, win_ref[:, c * cw:(c + 1) * cw], preferred_element_type=F32))
        v_sc[:, c * cw:(c + 1) * cw] = jax.nn.gelu(
            jnp.dot(h, win_ref[:, SGU_WIDTH + c * cw:SGU_WIDTH + (c + 1) * cw],
                    preferred_element_type=F32))
    v = v_sc[...]
    mu = jnp.mean(v, axis=-1, keepdims=True)
    vc = v - mu
    var = jnp.mean(vc * vc, axis=-1, keepdims=True)
    vn = vc * lax.rsqrt(var + EPS) * lng_ref[...] + lnb_ref[...]
    if emit_v:
        v_ref[...] = vn
    vb_sc[...] = vn.astype(BF16)
    for c in range(tm // CHUNK):
        rs = slice(c * CHUNK, (c + 1) * CHUNK)
        for g in range(SGU_GROUPS):
            cs = slice(g * SGU_GROUP_DIM, (g + 1) * SGU_GROUP_DIM)
            mixed = jnp.dot(mix_ref[g], vb_sc[rs, cs], preferred_element_type=F32) + bias_ref[:, cs]
            p_sc[rs, cs] = (u_sc[rs, cs] * mixed).astype(BF16)
    y_ref[...] = x + jnp.dot(p_sc[...], wout_ref[...], preferred_element_type=F32)


def _sgu_call(x, g, w_in, ln_g, ln_b, mix, bias, w_out, *, emit_v, name):
    n = x.shape[0]
    tm = SGU_TM
    assert n % tm == 0
    row_spec = pl.BlockSpec((tm, D_MODEL), lambda i: (i, 0))
    out_shape = [jax.ShapeDtypeStruct((n, D_MODEL), F32)]
    out_specs = [row_spec]
    if emit_v:
        out_shape.append(jax.ShapeDtypeStruct((n, SGU_WIDTH), F32))
        out_specs.append(pl.BlockSpec((tm, SGU_WIDTH), lambda i: (i, 0)))
    return pl.pallas_call(
        functools.partial(_sgu_kernel, tm=tm, emit_v=emit_v),
        grid=(n // tm,),
        in_specs=[
            row_spec,
            _const_spec((1, D_MODEL)),
            _const_spec((D_MODEL, 2 * SGU_WIDTH)),
            _const_spec((1, SGU_WIDTH)),
            _const_spec((1, SGU_WIDTH)),
            _const_spec((SGU_GROUPS, CHUNK, CHUNK)),
            _const_spec((CHUNK, SGU_WIDTH)),
            _const_spec((SGU_WIDTH, D_MODEL)),
        ],
        out_specs=out_specs,
        out_shape=out_shape,
        scratch_shapes=[
            pltpu.VMEM((tm, SGU_WIDTH), F32),
            pltpu.VMEM((tm, SGU_WIDTH), F32),
            pltpu.VMEM((tm, SGU_WIDTH), BF16),
            pltpu.VMEM((tm, SGU_WIDTH), BF16),
        ],
        compiler_params=pltpu.CompilerParams(
            dimension_semantics=("arbitrary",), vmem_limit_bytes=VMEM_LIMIT),
        name=name,
    )(x, g, w_in, ln_g, ln_b, mix, bias, w_out)


def _conv_gate(a, p1, p2, cw, cb):
    cv = cb + (cw[0:1] * p2 + cw[1:2] * p1 + cw[2:3] * a)
    gate = cv[:, :FF_CHUNK]
    val = cv[:, FF_CHUNK:]
    return (gate * jax.nn.sigmoid(gate) * val).astype(BF16)


def _ffn_prompt_kernel(x_ref, g_ref, wup_ref, cw_ref, cb_ref, wdown_ref,
                       y_ref, st_ref, carry_sc, acc_sc, *, tm):
    @pl.when(pl.program_id(1) == 0)
    def _():
        carry_sc[...] = jnp.zeros_like(carry_sc)

    x = x_ref[...]
    h = _rms(x, g_ref[...]).astype(BF16)
    row = lax.broadcasted_iota(jnp.int32, (tm, 1), 0)
    for j in range(N_FF_CHUNKS):
        cs = slice(j * 2 * FF_CHUNK, (j + 1) * 2 * FF_CHUNK)
        a = jnp.dot(h, wup_ref[:, cs], preferred_element_type=F32)
        c0 = carry_sc[SUBLANES - 2:SUBLANES - 1, cs]
        c1 = carry_sc[SUBLANES - 1:SUBLANES, cs]
        p1 = jnp.where(row == 0, c1, pltpu.roll(a, 1, 0))
        p2 = jnp.where(row == 0, c0, jnp.where(row == 1, c1, pltpu.roll(a, 2, 0)))
        carry_sc[:, cs] = a[tm - SUBLANES:tm, :]
        p = _conv_gate(a, p1, p2, cw_ref[:, cs], cb_ref[:, cs])
        contrib = jnp.dot(p, wdown_ref[j * FF_CHUNK:(j + 1) * FF_CHUNK, :],
                          preferred_element_type=F32)
        if j == 0:
            acc_sc[...] = contrib
        else:
            acc_sc[...] += contrib
    y_ref[...] = x + acc_sc[...]
    st_ref[...] = carry_sc[...]


def _ffn_prompt_call(x, g, w_up, conv_w, conv_b, w_down, *, name):
    b, length, _ = x.shape
    tm = FFN_TM
    assert length % tm == 0
    row_spec = pl.BlockSpec((None, tm, D_MODEL), lambda i, t: (i, t, 0))
    return pl.pallas_call(
        functools.partial(_ffn_prompt_kernel, tm=tm),
        grid=(b, length // tm),
        in_specs=[
            row_spec,
            _const_spec((1, D_MODEL)),
            _const_spec((D_MODEL, 2 * D_FF)),
            _const_spec((CONV_W, 2 * D_FF)),
            _const_spec((1, 2 * D_FF)),
            _const_spec((D_FF, D_MODEL)),
        ],
        out_specs=[
            row_spec,
            pl.BlockSpec((None, SUBLANES, 2 * D_FF), lambda i, t: (i, 0, 0)),
        ],
        out_shape=[
            jax.ShapeDtypeStruct((b, length, D_MODEL), F32),
            jax.ShapeDtypeStruct((b, SUBLANES, 2 * D_FF), F32),
        ],
        scratch_shapes=[
            pltpu.VMEM((SUBLANES, 2 * D_FF), F32),
            pltpu.VMEM((tm, D_MODEL), F32),
        ],
        compiler_params=pltpu.CompilerParams(
            dimension_semantics=("arbitrary", "arbitrary"), vmem_limit_bytes=VMEM_LIMIT),
        name=name,
    )(x, g, w_up, conv_w, conv_b, w_down)


def _ffn_sample_kernel(x_ref, g_ref, wup_ref, cw_ref, cb_ref, st_ref, wdown_ref,
                       y_ref, sto_ref, h_sc, acc_sc):
    j = pl.program_id(0)

    @pl.when(j == 0)
    def _():
        h_sc[...] = _rms(x_ref[...], g_ref[...]).astype(BF16)
        acc_sc[...] = jnp.zeros_like(acc_sc)

    n = DEC_BATCH * DEC_SEQ
    a = jnp.dot(h_sc[...], wup_ref[...], preferred_element_type=F32)
    s0 = st_ref[0]
    s1 = st_ref[1]
    p1 = jnp.concatenate([s1, a[:n - DEC_BATCH]], axis=0)
    p2 = jnp.concatenate([s0, s1, a[:n - 2 * DEC_BATCH]], axis=0)
    p = _conv_gate(a, p1, p2, cw_ref[...], cb_ref[...])
    acc_sc[...] += jnp.dot(p, wdown_ref[...], preferred_element_type=F32)
    sto_ref[0] = a[n - 2 * DEC_BATCH:n - DEC_BATCH]
    sto_ref[1] = a[n - DEC_BATCH:]

    @pl.when(j == N_FF_CHUNKS - 1)
    def _():
        y_ref[...] = x_ref[...] + acc_sc[...]


def _ffn_sample_call(x, g, w_up, conv_w, conv_b, state, w_down, *, name):
    n = DEC_BATCH * DEC_SEQ
    cw2 = 2 * FF_CHUNK
    return pl.pallas_call(
        _ffn_sample_kernel,
        grid=(N_FF_CHUNKS,),
        in_specs=[
            pl.BlockSpec((n, D_MODEL), lambda j: (0, 0)),
            pl.BlockSpec((1, D_MODEL), lambda j: (0, 0)),
            pl.BlockSpec((D_MODEL, cw2), lambda j: (0, j)),
            pl.BlockSpec((CONV_W, cw2), lambda j: (0, j)),
            pl.BlockSpec((1, cw2), lambda j: (0, j)),
            pl.BlockSpec((CONV_W - 1, DEC_BATCH, cw2), lambda j: (0, 0, j)),
            pl.BlockSpec((FF_CHUNK, D_MODEL), lambda j: (j, 0)),
        ],
        out_specs=[
            pl.BlockSpec((n, D_MODEL), lambda j: (0, 0)),
            pl.BlockSpec((CONV_W - 1, DEC_BATCH, cw2), lambda j: (0, 0, j)),
        ],
        out_shape=[
            jax.ShapeDtypeStruct((n, D_MODEL), F32),
            jax.ShapeDtypeStruct((CONV_W - 1, DEC_BATCH, 2 * D_FF), F32),
        ],
        scratch_shapes=[
            pltpu.VMEM((n, D_MODEL), BF16),
            pltpu.VMEM((n, D_MODEL), F32),
        ],
        compiler_params=pltpu.CompilerParams(
            dimension_semantics=("arbitrary",), vmem_limit_bytes=VMEM_LIMIT),
        name=name,
    )(x, g, w_up, conv_w, conv_b, state, w_down)


def _qk_norm_rope(xb, gain, cos, sin, lane):
    sq = xb * xb
    lo = jnp.sum(jnp.where(lane < HEAD_DIM, sq, 0.0), axis=-1, keepdims=True)
    hi = jnp.sum(jnp.where(lane >= HEAD_DIM, sq, 0.0), axis=-1, keepdims=True)
    ms = jnp.where(lane < HEAD_DIM, lo, hi) * (1.0 / HEAD_DIM)
    y = xb * lax.rsqrt(ms + EPS) * gain
    first_half = (lane & (HEAD_DIM - 1)) < HEAD_DIM // 2
    rot = jnp.where(first_half, pltpu.roll(y, LANES - HEAD_DIM // 2, 1),
                    pltpu.roll(y, HEAD_DIM // 2, 1))
    return y * cos + rot * sin


def _place_query_head(qb, hd, lane):
    kv_half = (hd // (N_HEADS // N_KV_HEADS)) % 2
    src = qb if hd % 2 == kv_half else pltpu.roll(qb, HEAD_DIM, 1)
    keep = (lane >= HEAD_DIM) if kv_half == 1 else (lane < HEAD_DIM)
    return jnp.where(keep, src, 0.0)


def _gather_head_pair(out_a, out_b, c, lane):
    kv_half = (c // 2) % 2
    if kv_half == 0:
        return jnp.where(lane < HEAD_DIM, out_a, pltpu.roll(out_b, HEAD_DIM, 1))
    return jnp.where(lane < HEAD_DIM, pltpu.roll(out_a, HEAD_DIM, 1), out_b)


def _softmax_with_sink(lg, valid, sink):
    lg = jnp.where(valid, lg, -jnp.inf)
    m = jnp.maximum(jnp.max(lg, axis=-1, keepdims=True), sink)
    p = jnp.exp(lg - m)
    den = jnp.sum(p, axis=-1, keepdims=True) + jnp.exp(sink - m)
    return (p / den).astype(BF16)


_NT = (((1,), (1,)), ((), ()))
HALF_HEADS = N_HEADS // 2


def _swa_prompt_kernel(x_ref, g_ref, wqkv_ref, qn_ref, kn_ref, cos_ref, sin_ref, sink_ref,
                       wo_ref, y_ref, kc_ref, vc_ref,
                       kband_sc, vband_sc, qexp_sc, lg_sc, p_sc, out_sc, o_sc, *, tq):
    n = pl.program_id(1)
    nb = tq // WINDOW

    @pl.when(n == 0)
    def _():
        kband_sc[...] = jnp.zeros_like(kband_sc)
        vband_sc[...] = jnp.zeros_like(vband_sc)

    x = x_ref[...]
    h = _rms(x, g_ref[...]).astype(BF16)
    qkv = jnp.dot(h, wqkv_ref[...], preferred_element_type=F32)
    lane = lax.broadcasted_iota(jnp.int32, (1, LANES), 1)
    tq_idx = lax.broadcasted_iota(jnp.int32, (WINDOW, 1), 0)
    key_idx = lax.broadcasted_iota(jnp.int32, (1, 2 * WINDOW), 1)
    in_window = (key_idx > tq_idx) & (key_idx <= tq_idx + WINDOW)
    scale = HEAD_DIM ** -0.5
    for blk in range(nb):
        rs = slice(blk * WINDOW, (blk + 1) * WINDOW)
        cos = cos_ref[rs, :]
        sin = sin_ref[rs, :]
        kband_sc[0:WINDOW, :] = kband_sc[WINDOW:2 * WINDOW, :]
        vband_sc[0:WINDOW, :] = vband_sc[WINDOW:2 * WINDOW, :]
        k_cur = jnp.concatenate(
            [_qk_norm_rope(qkv[rs, D_MODEL + c * LANES:D_MODEL + (c + 1) * LANES],
                           kn_ref[...], cos, sin, lane) for c in range(KV_WIDTH // LANES)], axis=1)
        v_cur = qkv[rs, D_MODEL + KV_WIDTH:]
        kband_sc[WINDOW:2 * WINDOW, :] = k_cur.astype(BF16)
        vband_sc[WINDOW:2 * WINDOW, :] = v_cur.astype(BF16)
        if blk == nb - 1:
            kc_ref[...] = k_cur
            vc_ref[...] = v_cur
        for c in range(D_MODEL // LANES):
            qb = _qk_norm_rope(qkv[rs, c * LANES:(c + 1) * LANES], qn_ref[...], cos, sin, lane) * scale
            for hd in (2 * c, 2 * c + 1):
                qexp_sc[hd * WINDOW:(hd + 1) * WINDOW, :] = _place_query_head(qb, hd, lane).astype(BF16)
        rows_half = HALF_HEADS * WINDOW
        for half in range(2):
            hs = slice(half * rows_half, (half + 1) * rows_half)
            lg_sc[hs, :] = lax.dot_general(
                qexp_sc[hs, :], kband_sc[:, half * LANES:(half + 1) * LANES], _NT,
                preferred_element_type=F32)
        first_key = jnp.where(n * nb + blk > 0, 0, WINDOW)
        valid = in_window & (key_idx >= first_key)
        for hd in range(N_HEADS):
            hr = slice(hd * WINDOW, (hd + 1) * WINDOW)
            p_sc[hr, :] = _softmax_with_sink(lg_sc[hr, :], valid, sink_ref[hd])
        for half in range(2):
            hs = slice(half * rows_half, (half + 1) * rows_half)
            out_sc[hs, :] = jnp.dot(p_sc[hs, :], vband_sc[:, half * LANES:(half + 1) * LANES],
                                    preferred_element_type=F32)
        for c in range(D_MODEL // LANES):
            oa = out_sc[(2 * c) * WINDOW:(2 * c + 1) * WINDOW, :]
            ob = out_sc[(2 * c + 1) * WINDOW:(2 * c + 2) * WINDOW, :]
            o_sc[rs, c * LANES:(c + 1) * LANES] = _gather_head_pair(oa, ob, c, lane).astype(BF16)
    y_ref[...] = x + jnp.dot(o_sc[...], wo_ref[...], preferred_element_type=F32)


def _swa_prompt_call(x, g, w_qkv, qn, kn, cos, sin, sinks, w_o, *, name):
    b, length, _ = x.shape
    tq = SWA_TQ
    assert length % tq == 0
    row_spec = pl.BlockSpec((None, tq, D_MODEL), lambda i, t: (i, t, 0))
    tab_spec = pl.BlockSpec((tq, LANES), lambda i, t: (t, 0))
    cache_spec = pl.BlockSpec((None, WINDOW, KV_WIDTH), lambda i, t: (i, 0, 0))
    return pl.pallas_call(
        functools.partial(_swa_prompt_kernel, tq=tq),
        grid=(b, length // tq),
        in_specs=[
            row_spec,
            _const_spec((1, D_MODEL)),
            _const_spec((D_MODEL, QKV_WIDTH)),
            _const_spec((1, LANES)),
            _const_spec((1, LANES)),
            tab_spec,
            tab_spec,
            pl.BlockSpec(memory_space=pltpu.SMEM),
            _const_spec((D_MODEL, D_MODEL)),
        ],
        out_specs=[row_spec, cache_spec, cache_spec],
        out_shape=[
            jax.ShapeDtypeStruct((b, length, D_MODEL), F32),
            jax.ShapeDtypeStruct((b, WINDOW, KV_WIDTH), F32),
            jax.ShapeDtypeStruct((b, WINDOW, KV_WIDTH), F32),
        ],
        scratch_shapes=[
            pltpu.VMEM((2 * WINDOW, KV_WIDTH), BF16),
            pltpu.VMEM((2 * WINDOW, KV_WIDTH), BF16),
            pltpu.VMEM((N_HEADS * WINDOW, LANES), BF16),
            pltpu.VMEM((N_HEADS * WINDOW, 2 * WINDOW), F32),
            pltpu.VMEM((N_HEADS * WINDOW, 2 * WINDOW), BF16),
            pltpu.VMEM((N_HEADS * WINDOW, LANES), F32),
            pltpu.VMEM((tq, D_MODEL), BF16),
        ],
        compiler_params=pltpu.CompilerParams(
            dimension_semantics=("arbitrary", "arbitrary"), vmem_limit_bytes=VMEM_LIMIT),
        name=name,
    )(x, g, w_qkv, qn, kn, cos, sin, sinks, w_o)


def _swa_sample_kernel(x_ref, g_ref, wqkv_ref, qn_ref, kn_ref, cos_ref, sin_ref, sink_ref,
                       ck_ref, cv_ref, wo_ref, y_ref, ko_ref, vo_ref,
                       q_sc, knew_sc, vnew_sc, kall_sc, vall_sc, o_sc):
    pad_rows = slice(WINDOW + DEC_SEQ, 2 * WINDOW)

    @pl.when(pl.program_id(0) == 0)
    def _():
        kall_sc[pad_rows, :] = jnp.zeros((WINDOW - DEC_SEQ, KV_WIDTH), F32)
        vall_sc[pad_rows, :] = jnp.zeros((WINDOW - DEC_SEQ, KV_WIDTH), F32)

    x = x_ref[...]
    h = _rms(x, g_ref[...]).astype(BF16)
    qkv = jnp.dot(h, wqkv_ref[...], preferred_element_type=F32)
    lane = lax.broadcasted_iota(jnp.int32, (1, LANES), 1)
    cos = cos_ref[...]
    sin = sin_ref[...]
    scale = HEAD_DIM ** -0.5
    for c in range(D_MODEL // LANES):
        q_sc[:, c * LANES:(c + 1) * LANES] = _qk_norm_rope(
            qkv[:, c * LANES:(c + 1) * LANES], qn_ref[...], cos, sin, lane) * scale
    for c in range(KV_WIDTH // LANES):
        knew_sc[:, c * LANES:(c + 1) * LANES] = _qk_norm_rope(
            qkv[:, D_MODEL + c * LANES:D_MODEL + (c + 1) * LANES], kn_ref[...], cos, sin, lane)
    vnew_sc[...] = qkv[:, D_MODEL + KV_WIDTH:]

    t_idx = lax.broadcasted_iota(jnp.int32, (N_HEADS * DEC_SEQ, 1), 0) & (DEC_SEQ - 1)
    key_idx = lax.broadcasted_iota(jnp.int32, (1, 2 * WINDOW), 1)
    valid = ((key_idx < WINDOW) & (key_idx > t_idx)) | (
        (key_idx >= WINDOW) & (key_idx - WINDOW <= t_idx))
    sink = sink_ref[...]
    rows_half = HALF_HEADS * DEC_SEQ

    def per_batch(b, carry):
        off = pl.multiple_of(b * DEC_SEQ, DEC_SEQ)
        kall_sc[0:WINDOW, :] = ck_ref[b]
        vall_sc[0:WINDOW, :] = cv_ref[b]
        kall_sc[WINDOW:WINDOW + DEC_SEQ, :] = knew_sc[pl.ds(off, DEC_SEQ), :]
        vall_sc[WINDOW:WINDOW + DEC_SEQ, :] = vnew_sc[pl.ds(off, DEC_SEQ), :]
        ko_ref[b] = kall_sc[DEC_SEQ:WINDOW + DEC_SEQ, :]
        vo_ref[b] = vall_sc[DEC_SEQ:WINDOW + DEC_SEQ, :]
        qb = q_sc[pl.ds(off, DEC_SEQ), :]
        pieces = [_place_query_head(qb[:, (hd // 2) * LANES:(hd // 2 + 1) * LANES], hd, lane)
                  for hd in range(N_HEADS)]
        lgs = []
        for half in range(2):
            qe = jnp.concatenate(pieces[half * HALF_HEADS:(half + 1) * HALF_HEADS], axis=0)
            kb = kall_sc[:, half * LANES:(half + 1) * LANES].astype(BF16)
            lgs.append(lax.dot_general(qe.astype(BF16), kb, _NT, preferred_element_type=F32))
        p = _softmax_with_sink(jnp.concatenate(lgs, axis=0), valid, sink)
        outs = []
        for half in range(2):
            vb = vall_sc[:, half * LANES:(half + 1) * LANES].astype(BF16)
            outs.append(jnp.dot(p[half * rows_half:(half + 1) * rows_half, :], vb,
                                preferred_element_type=F32))
        out = jnp.concatenate(outs, axis=0)
        o_blocks = []
        for c in range(D_MODEL // LANES):
            oa = out[(2 * c) * DEC_SEQ:(2 * c + 1) * DEC_SEQ, :]
            ob = out[(2 * c + 1) * DEC_SEQ:(2 * c + 2) * DEC_SEQ, :]
            o_blocks.append(_gather_head_pair(oa, ob, c, lane))
        o_sc[pl.ds(off, DEC_SEQ), :] = jnp.concatenate(o_blocks, axis=1)
        return carry

    lax.fori_loop(0, SWA_TB, per_batch, 0)
    y_ref[...] = x + jnp.dot(o_sc[...].astype(BF16), wo_ref[...], preferred_element_type=F32)


def _swa_sample_call(x, g, w_qkv, qn, kn, cos, sin, sink_col, cache_k, cache_v, w_o, *, name):
    n = x.shape[0]
    tm = SWA_TB * DEC_SEQ
    row_spec = pl.BlockSpec((tm, D_MODEL), lambda i: (i, 0))
    cache_spec = pl.BlockSpec((SWA_TB, WINDOW, KV_WIDTH), lambda i: (i, 0, 0))
    return pl.pallas_call(
        _swa_sample_kernel,
        grid=(n // tm,),
        in_specs=[
            row_spec,
            _const_spec((1, D_MODEL)),
            _const_spec((D_MODEL, QKV_WIDTH)),
            _const_spec((1, LANES)),
            _const_spec((1, LANES)),
            _const_spec((tm, LANES)),
            _const_spec((tm, LANES)),
            _const_spec((N_HEADS * DEC_SEQ, 1)),
            cache_spec,
            cache_spec,
            _const_spec((D_MODEL, D_MODEL)),
        ],
        out_specs=[row_spec, cache_spec, cache_spec],
        out_shape=[
            jax.ShapeDtypeStruct((n, D_MODEL), F32),
            jax.ShapeDtypeStruct((DEC_BATCH, WINDOW, KV_WIDTH), F32),
            jax.ShapeDtypeStruct((DEC_BATCH, WINDOW, KV_WIDTH), F32),
        ],
        scratch_shapes=[
            pltpu.VMEM((tm, D_MODEL), F32),
            pltpu.VMEM((tm, KV_WIDTH), F32),
            pltpu.VMEM((tm, KV_WIDTH), F32),
            pltpu.VMEM((2 * WINDOW, KV_WIDTH), F32),
            pltpu.VMEM((2 * WINDOW, KV_WIDTH), F32),
            pltpu.VMEM((tm, D_MODEL), F32),
        ],
        compiler_params=pltpu.CompilerParams(
            dimension_semantics=("arbitrary",), vmem_limit_bytes=VMEM_LIMIT),
        name=name,
    )(x, g, w_qkv, qn, kn, cos, sin, sink_col, cache_k, cache_v, w_o)


def _permute_ff(a):
    lead = a.shape[:-1]
    a = a.reshape(lead + (2, N_FF_CHUNKS, FF_CHUNK))
    return jnp.swapaxes(a, -3, -2).reshape(lead + (2 * D_FF,))


def _unpermute_ff(a):
    lead = a.shape[:-1]
    a = a.reshape(lead + (N_FF_CHUNKS, 2, FF_CHUNK))
    return jnp.swapaxes(a, -3, -2).reshape(lead + (2 * D_FF,))


def _rope_tables(pos):
    inv_freq = jnp.power(jnp.float32(ROPE_THETA),
                         -jnp.arange(0, HEAD_DIM, 2, dtype=F32) / HEAD_DIM)
    ang = pos.astype(F32)[:, None] * inv_freq[None, :]
    reps = LANES // (HEAD_DIM // 2)
    cos = jnp.tile(jnp.cos(ang), (1, reps))
    lane = jnp.arange(LANES)
    sign = jnp.where((lane % HEAD_DIM) < HEAD_DIM // 2, -1.0, 1.0).astype(F32)
    sin = jnp.tile(jnp.sin(ang), (1, reps)) * sign[None, :]
    return cos, sin


def kernel(x_prompt, x_sample, cache_swa_k, cache_swa_v, state_ffn_conv, mix_norm_g, sgu_w_in,
           sgu_ln_g, sgu_ln_b, sgu_w_s, sgu_b_s, sgu_w_out, attn_w_qkv, attn_q_norm, attn_k_norm,
           attn_sinks, attn_w_o, ffn_norm_g, ffn_w_up, ffn_conv_w, ffn_conv_b, ffn_w_down):
    n_s = DEC_BATCH * DEC_SEQ

    def to_time_major(a):
        return a.reshape(DEC_BATCH, DEC_SEQ, -1).swapaxes(0, 1).reshape(n_s, -1)

    def to_batch_major(a):
        return a.reshape(DEC_SEQ, DEC_BATCH, -1).swapaxes(0, 1).reshape(n_s, -1)

    w_s = jnp.where(jnp.tril(jnp.ones((CHUNK, CHUNK), dtype=bool))[None], sgu_w_s[0], 0.0)
    mix_p = w_s.astype(BF16)
    eye = jnp.eye(CHUNK // DEC_SEQ, dtype=F32)
    mix_s = jnp.einsum('ab,gts->gatbs', eye, w_s[:, :DEC_SEQ, :DEC_SEQ]).reshape(
        SGU_GROUPS, CHUNK, CHUNK).astype(BF16)
    bias_p = jnp.repeat(sgu_b_s[0].T, SGU_GROUP_DIM, axis=1)
    bias_s = jnp.tile(bias_p[:DEC_SEQ], (CHUNK // DEC_SEQ, 1))
    sgu_args = (mix_norm_g[0][None], sgu_w_in[0].astype(BF16), sgu_ln_g[0][None], sgu_ln_b[0][None])
    w_out = sgu_w_out[0].astype(BF16)
    (yp,) = _sgu_call(x_prompt.reshape(BATCH * SEQ, D_MODEL), *sgu_args, mix_p, bias_p, w_out,
                      emit_v=False, name="sgu_prompt")
    ys, v_s = _sgu_call(x_sample.reshape(n_s, D_MODEL), *sgu_args, mix_s, bias_s, w_out,
                        emit_v=True, name="sgu_sample")
    yp = yp.reshape(BATCH, SEQ, D_MODEL)

    conv_p, conv_s = [], []

    def conv_ffn(layer, yp, ys):
        w_up = _permute_ff(ffn_w_up[layer]).astype(BF16)
        cw = _permute_ff(ffn_conv_w[layer])
        cb = _permute_ff(ffn_conv_b[layer])[None]
        w_down = ffn_w_down[layer].astype(BF16)
        g = ffn_norm_g[layer][None]
        yp, st_p = _ffn_prompt_call(yp, g, w_up, cw, cb, w_down, name=f"ffn_prompt_{layer}")
        st_in = _permute_ff(state_ffn_conv[layer]).swapaxes(0, 1)
        ys_t, st_s = _ffn_sample_call(to_time_major(ys), g, w_up, cw, cb, st_in, w_down,
                                      name=f"ffn_sample_{layer}")
        conv_p.append(_unpermute_ff(st_p[:, SUBLANES - (CONV_W - 1):, :]))
        conv_s.append(_unpermute_ff(st_s).swapaxes(0, 1))
        return yp, to_batch_major(ys_t)

    yp, ys = conv_ffn(0, yp, ys)

    w_qkv = attn_w_qkv[0].astype(BF16)
    w_o = attn_w_o[0].astype(BF16)
    qn = jnp.tile(attn_q_norm[0], LANES // HEAD_DIM)[None]
    kn = jnp.tile(attn_k_norm[0], LANES // HEAD_DIM)[None]
    g = mix_norm_g[1][None]
    cos_p, sin_p = _rope_tables(jnp.arange(SEQ, dtype=jnp.int32))
    yp, k_p, v_p = _swa_prompt_call(yp, g, w_qkv, qn, kn, cos_p, sin_p, attn_sinks[0], w_o,
                                    name="swa_prompt")
    pos_s = PAST_LEN + (jnp.arange(SWA_TB * DEC_SEQ, dtype=jnp.int32) % DEC_SEQ)
    cos_s, sin_s = _rope_tables(pos_s)
    sink_col = jnp.repeat(attn_sinks[0], DEC_SEQ)[:, None]
    ys, k_s, v_s_cache = _swa_sample_call(
        ys, g, w_qkv, qn, kn, cos_s, sin_s, sink_col,
        cache_swa_k[0].reshape(DEC_BATCH, WINDOW, KV_WIDTH),
        cache_swa_v[0].reshape(DEC_BATCH, WINDOW, KV_WIDTH), w_o, name="swa_sample")

    yp, ys = conv_ffn(1, yp, ys)

    cache_shape_p = (1, BATCH, WINDOW, N_KV_HEADS, HEAD_DIM)
    cache_shape_s = (1, DEC_BATCH, WINDOW, N_KV_HEADS, HEAD_DIM)
    return (yp,
            ys.reshape(DEC_BATCH, DEC_SEQ, D_MODEL),
            v_s.reshape(1, DEC_BATCH, DEC_SEQ, SGU_WIDTH),
            k_p.reshape(cache_shape_p),
            v_p.reshape(cache_shape_p),
            k_s.reshape(cache_shape_s),
            v_s_cache.reshape(cache_shape_s),
            jnp.stack(conv_p),
            jnp.stack(conv_s))
```

```python
import functools

import jax
import jax.numpy as jnp
from jax import lax
from jax.experimental import pallas as pl
from jax.experimental.pallas import tpu as pltpu

F32 = jnp.float32
BF16 = jnp.bfloat16

D_MODEL = 1024
BATCH = 8
SEQ = 2048
DEC_BATCH = 128
DEC_SEQ = 8
PAST_LEN = 8192
CHUNK = 128
SGU_WIDTH = 2 * D_MODEL
SGU_GROUPS = 8
SGU_GROUP_DIM = SGU_WIDTH // SGU_GROUPS
HEAD_DIM = 64
N_HEADS = D_MODEL // HEAD_DIM
N_KV_HEADS = 4
KV_WIDTH = N_KV_HEADS * HEAD_DIM
QKV_WIDTH = D_MODEL + 2 * KV_WIDTH
WINDOW = 128
ROPE_THETA = 10000.0
D_FF = 2816
CONV_W = 3
EPS = 1e-6

LANES = 128
SUBLANES = 8
FF_CHUNK = 256
N_FF_CHUNKS = D_FF // FF_CHUNK
VMEM_LIMIT = 56 * 1024 * 1024

SGU_TM = 256
FFN_TM = 512
SWA_TQ = 512
SWA_TB = 16


def _const_spec(shape):
    nd = len(shape)
    return pl.BlockSpec(shape, lambda *_: (0,) * nd, pipeline_mode=pl.Buffered(1))


def _rms(x, g):
    ms = jnp.mean(x * x, axis=-1, keepdims=True)
    return x * lax.rsqrt(ms + EPS) * g


def _sgu_kernel(x_ref, g_ref, win_ref, lng_ref, lnb_ref, mix_ref, bias_ref, wout_ref,
                *rest, tm, emit_v):
    if emit_v:
        y_ref, v_ref, u_sc, v_sc, vb_sc, p_sc = rest
    else:
        y_ref, u_sc, v_sc, vb_sc, p_sc = rest
        v_ref = None
    x = x_ref[...]
    h = _rms(x, g_ref[...]).astype(BF16)
    cw = 512
    for c in range(SGU_WIDTH // cw):
        u_sc[:, c * cw:(c + 1) * cw] = jax.nn.gelu(
            jnp.dot(h, win_ref[:, c * cw:(c + 1) * cw], preferred_element_type=F32))
        v_sc[:, c * cw:(c + 1) * cw] = jax.nn.gelu(
            jnp.dot(h, win_ref[:, SGU_WIDTH + c * cw:SGU_WIDTH + (c + 1) * cw],
                    preferred_element_type=F32))
    v = v_sc[...]
    mu = jnp.mean(v, axis=-1, keepdims=True)
    vc = v - mu
    var = jnp.mean(vc * vc, axis=-1, keepdims=True)
    vn = vc * lax.rsqrt(var + EPS) * lng_ref[...] + lnb_ref[...]
    if emit_v:
        v_ref[...] = vn
    vb_sc[...] = vn.astype(BF16)
    for c in range(tm // CHUNK):
        rs = slice(c * CHUNK, (c + 1) * CHUNK)
        for g in range(SGU_GROUPS):
            cs = slice(g * SGU_GROUP_DIM, (g + 1) * SGU_GROUP_DIM)
            mixed = jnp.dot(mix_ref[g], vb_sc[rs, cs], preferred_element_type=F32) + bias_ref[:, cs]
            p_sc[rs, cs] = (u_sc[rs, cs] * mixed).astype(BF16)
    y_ref[...] = x + jnp.dot(p_sc[...], wout_ref[...], preferred_element_type=F32)


def _sgu_call(x, g, w_in, ln_g, ln_b, mix, bias, w_out, *, emit_v, name):
    n = x.shape[0]
    tm = SGU_TM
    assert n % tm == 0
    row_spec = pl.BlockSpec((tm, D_MODEL), lambda i: (i, 0))
    out_shape = [jax.ShapeDtypeStruct((n, D_MODEL), F32)]
    out_specs = [row_spec]
    if emit_v:
        out_shape.append(jax.ShapeDtypeStruct((n, SGU_WIDTH), F32))
        out_specs.append(pl.BlockSpec((tm, SGU_WIDTH), lambda i: (i, 0)))
    return pl.pallas_call(
        functools.partial(_sgu_kernel, tm=tm, emit_v=emit_v),
        grid=(n // tm,),
        in_specs=[
            row_spec,
            _const_spec((1, D_MODEL)),
            _const_spec((D_MODEL, 2 * SGU_WIDTH)),
            _const_spec((1, SGU_WIDTH)),
            _const_spec((1, SGU_WIDTH)),
            _const_spec((SGU_GROUPS, CHUNK, CHUNK)),
            _const_spec((CHUNK, SGU_WIDTH)),
            _const_spec((SGU_WIDTH, D_MODEL)),
        ],
        out_specs=out_specs,
        out_shape=out_shape,
        scratch_shapes=[
            pltpu.VMEM((tm, SGU_WIDTH), F32),
            pltpu.VMEM((tm, SGU_WIDTH), F32),
            pltpu.VMEM((tm, SGU_WIDTH), BF16),
            pltpu.VMEM((tm, SGU_WIDTH), BF16),
        ],
        compiler_params=pltpu.CompilerParams(
            dimension_semantics=("arbitrary",), vmem_limit_bytes=VMEM_LIMIT),
        name=name,
    )(x, g, w_in, ln_g, ln_b, mix, bias, w_out)


def _ffn_cols(j, part):
    lo = part * D_FF + j * FF_CHUNK
    return slice(lo, lo + FF_CHUNK)


def _ffn_kernel(*refs, tm, sample):
    if sample:
        x_ref, g_ref, wup_ref, cw_ref, cb_ref, wdown_ref, sti_ref, y_ref, st_ref, p_sc = refs
    else:
        x_ref, g_ref, wup_ref, cw_ref, cb_ref, wdown_ref, y_ref, st_ref, carry_sc, p_sc = refs

        @pl.when(pl.program_id(1) == 0)
        def _():
            carry_sc[...] = jnp.zeros_like(carry_sc)

    x = x_ref[...]
    h = _rms(x, g_ref[...]).astype(BF16)

    def up(j):
        return [jnp.dot(h, wup_ref[:, _ffn_cols(j, part)], preferred_element_type=F32)
                for part in range(2)]

    def taps(cs):
        return cw_ref[0:1, cs], cw_ref[1:2, cs], cw_ref[2:3, cs], cb_ref[:, cs]

    def conv_prompt(a, cs):
        w0, w1, w2, cb = taps(cs)
        sub = lax.broadcasted_iota(jnp.int32, (SUBLANES, 1), 0)
        c0 = carry_sc[SUBLANES - 2:SUBLANES - 1, cs]
        c1 = carry_sc[SUBLANES - 1:SUBLANES, cs]
        r1 = pltpu.roll(a, 1, 0)
        r2 = pltpu.roll(a, 2, 0)
        h1 = jnp.where(sub == 0, c1, r1[:SUBLANES])
        h2 = jnp.where(sub == 0, c0, jnp.where(sub == 1, c1, r2[:SUBLANES]))
        p1 = jnp.concatenate([h1, r1[SUBLANES:]], axis=0)
        p2 = jnp.concatenate([h2, r2[SUBLANES:]], axis=0)
        carry_sc[:, cs] = a[tm - SUBLANES:, :]
        return cb + (w0 * p2 + w1 * p1 + w2 * a)

    def conv_sample(a, cs):
        w0, w1, w2, cb = [v[None] for v in taps(cs)]
        cs1 = slice(cs.start + 2 * D_FF, cs.stop + 2 * D_FF)
        a3 = a.reshape(tm // DEC_SEQ, DEC_SEQ, FF_CHUNK)
        t = lax.broadcasted_iota(jnp.int32, (1, DEC_SEQ, 1), 1)
        s0 = jnp.broadcast_to(sti_ref[:, cs][:, None, :], a3.shape)
        s1 = jnp.broadcast_to(sti_ref[:, cs1][:, None, :], a3.shape)
        p1 = jnp.where(t == 0, s1, pltpu.roll(a3, 1, 1))
        p2 = jnp.where(t == 0, s0, jnp.where(t == 1, s1, pltpu.roll(a3, 2, 1)))
        st_ref[:, cs] = a3[:, DEC_SEQ - 2, :]
        st_ref[:, cs1] = a3[:, DEC_SEQ - 1, :]
        return (cb + (w0 * p2 + w1 * p1 + w2 * a3)).reshape(tm, FF_CHUNK)

    conv = conv_sample if sample else conv_prompt
    nxt = up(0)
    for j in range(N_FF_CHUNKS):
        cur = nxt
        if j + 1 < N_FF_CHUNKS:
            nxt = up(j + 1)
        gate = conv(cur[0], _ffn_cols(j, 0))
        val = conv(cur[1], _ffn_cols(j, 1))
        p_sc[:, j * FF_CHUNK:(j + 1) * FF_CHUNK] = (gate * jax.nn.sigmoid(gate) * val).astype(BF16)
    y_ref[...] = x + jnp.dot(p_sc[...], wdown_ref[...], preferred_element_type=F32)
    if not sample:
        st_ref[...] = carry_sc[...]


def _ffn_call(x, g, w_up, conv_w, conv_b, w_down, state=None, *, name):
    sample = state is not None
    tm = FFN_TM
    weight_specs = [
        _const_spec((1, D_MODEL)),
        _const_spec((D_MODEL, 2 * D_FF)),
        _const_spec((CONV_W, 2 * D_FF)),
        _const_spec((1, 2 * D_FF)),
        _const_spec((D_FF, D_MODEL)),
    ]
    scratch = [pltpu.VMEM((tm, D_FF), BF16)]
    if sample:
        assert DEC_SEQ == SUBLANES
        n = x.shape[0]
        assert n % tm == 0
        grid = (n // tm,)
        row_spec = pl.BlockSpec((tm, D_MODEL), lambda i: (i, 0))
        st_spec = pl.BlockSpec((tm // DEC_SEQ, (CONV_W - 1) * 2 * D_FF), lambda i: (i, 0))
        in_specs = [row_spec] + weight_specs + [st_spec]
        st_shape = jax.ShapeDtypeStruct(state.shape, F32)
        args = (x, g, w_up, conv_w, conv_b, w_down, state)
    else:
        b, length, _ = x.shape
        assert length % tm == 0
        grid = (b, length // tm)
        row_spec = pl.BlockSpec((None, tm, D_MODEL), lambda i, t: (i, t, 0))
        st_spec = pl.BlockSpec((None, SUBLANES, 2 * D_FF), lambda i, t: (i, 0, 0))
        in_specs = [row_spec] + weight_specs
        st_shape = jax.ShapeDtypeStruct((b, SUBLANES, 2 * D_FF), F32)
        scratch = [pltpu.VMEM((SUBLANES, 2 * D_FF), F32)] + scratch
        args = (x, g, w_up, conv_w, conv_b, w_down)
    return pl.pallas_call(
        functools.partial(_ffn_kernel, tm=tm, sample=sample),
        grid=grid,
        in_specs=in_specs,
        out_specs=[row_spec, st_spec],
        out_shape=[jax.ShapeDtypeStruct(x.shape, F32), st_shape],
        scratch_shapes=scratch,
        compiler_params=pltpu.CompilerParams(
            dimension_semantics=("arbitrary",) * len(grid), vmem_limit_bytes=VMEM_LIMIT),
        name=name,
    )(*args)


def _qk_norm_rope(xb, gain, cos, sin, lane):
    sq = xb * xb
    lo = jnp.sum(jnp.where(lane < HEAD_DIM, sq, 0.0), axis=-1, keepdims=True)
    hi = jnp.sum(jnp.where(lane >= HEAD_DIM, sq, 0.0), axis=-1, keepdims=True)
    ms = jnp.where(lane < HEAD_DIM, lo, hi) * (1.0 / HEAD_DIM)
    y = xb * lax.rsqrt(ms + EPS) * gain
    first_half = (lane & (HEAD_DIM - 1)) < HEAD_DIM // 2
    rot = jnp.where(first_half, pltpu.roll(y, LANES - HEAD_DIM // 2, 1),
                    pltpu.roll(y, HEAD_DIM // 2, 1))
    return y * cos + rot * sin


def _place_query_head(qb, hd, lane):
    kv_half = (hd // (N_HEADS // N_KV_HEADS)) % 2
    src = qb if hd % 2 == kv_half else pltpu.roll(qb, HEAD_DIM, 1)
    keep = (lane >= HEAD_DIM) if kv_half == 1 else (lane < HEAD_DIM)
    return jnp.where(keep, src, 0.0)


def _gather_head_pair(out_a, out_b, c, lane):
    kv_half = (c // 2) % 2
    if kv_half == 0:
        return jnp.where(lane < HEAD_DIM, out_a, pltpu.roll(out_b, HEAD_DIM, 1))
    return jnp.where(lane < HEAD_DIM, pltpu.roll(out_a, HEAD_DIM, 1), out_b)


def _softmax_with_sink(lg, valid, sink):
    lg = jnp.where(valid, lg, -jnp.inf)
    m = jnp.maximum(jnp.max(lg, axis=-1, keepdims=True), sink)
    p = jnp.exp(lg - m)
    den = jnp.sum(p, axis=-1, keepdims=True) + jnp.exp(sink - m)
    return (p / den).astype(BF16)


_NT = (((1,), (1,)), ((), ()))
HALF_HEADS = N_HEADS // 2


def _swa_prompt_kernel(x_ref, g_ref, wqkv_ref, qn_ref, kn_ref, cos_ref, sin_ref, sink_ref,
                       wo_ref, y_ref, kc_ref, vc_ref,
                       kall_sc, vall_sc, qexp_sc, lg_sc, p_sc, out_sc, o_sc, *, tq):
    n = pl.program_id(1)
    nb = tq // WINDOW
    prev = slice(0, WINDOW)
    last = slice(nb * WINDOW, (nb + 1) * WINDOW)

    @pl.when(n == 0)
    def _():
        kall_sc[prev, :] = jnp.zeros((WINDOW, KV_WIDTH), BF16)
        vall_sc[prev, :] = jnp.zeros((WINDOW, KV_WIDTH), BF16)

    @pl.when(n > 0)
    def _():
        kall_sc[prev, :] = kall_sc[last, :]
        vall_sc[prev, :] = vall_sc[last, :]

    x = x_ref[...]
    h = _rms(x, g_ref[...]).astype(BF16)
    qkv = jnp.dot(h, wqkv_ref[...], preferred_element_type=F32)
    lane = lax.broadcasted_iota(jnp.int32, (1, LANES), 1)
    cos = cos_ref[...]
    sin = sin_ref[...]
    scale = HEAD_DIM ** -0.5

    k_new = jnp.concatenate(
        [_qk_norm_rope(qkv[:, D_MODEL + c * LANES:D_MODEL + (c + 1) * LANES],
                       kn_ref[...], cos, sin, lane) for c in range(KV_WIDTH // LANES)], axis=1)
    v_new = qkv[:, D_MODEL + KV_WIDTH:]
    kall_sc[WINDOW:, :] = k_new.astype(BF16)
    vall_sc[WINDOW:, :] = v_new.astype(BF16)
    kc_ref[...] = k_new[tq - WINDOW:, :]
    vc_ref[...] = v_new[tq - WINDOW:, :]
    for c in range(D_MODEL // LANES):
        qb = _qk_norm_rope(qkv[:, c * LANES:(c + 1) * LANES], qn_ref[...], cos, sin, lane) * scale
        for hd in (2 * c, 2 * c + 1):
            placed = _place_query_head(qb, hd, lane).astype(BF16)
            for blk in range(nb):
                qexp_sc[blk, hd * WINDOW:(hd + 1) * WINDOW, :] = placed[blk * WINDOW:(blk + 1) * WINDOW, :]

    tq_idx = lax.broadcasted_iota(jnp.int32, (WINDOW, 1), 0)
    key_idx = lax.broadcasted_iota(jnp.int32, (1, 2 * WINDOW), 1)
    in_window = (key_idx > tq_idx) & (key_idx <= tq_idx + WINDOW)
    rows_half = HALF_HEADS * WINDOW

    def logits(blk):
        band = slice(blk * WINDOW, (blk + 2) * WINDOW)
        for half in range(2):
            hs = slice(half * rows_half, (half + 1) * rows_half)
            lg_sc[blk % 2, hs, :] = lax.dot_general(
                qexp_sc[blk, hs, :], kall_sc[band, half * LANES:(half + 1) * LANES], _NT,
                preferred_element_type=F32)

    logits(0)
    for blk in range(nb):
        if blk + 1 < nb:
            logits(blk + 1)
        first_key = jnp.where(n * nb + blk > 0, 0, WINDOW)
        valid = in_window & (key_idx >= first_key)
        for hd in range(N_HEADS):
            hr = slice(hd * WINDOW, (hd + 1) * WINDOW)
            p_sc[hr, :] = _softmax_with_sink(lg_sc[blk % 2, hr, :], valid, sink_ref[hd])
        band = slice(blk * WINDOW, (blk + 2) * WINDOW)
        for half in range(2):
            hs = slice(half * rows_half, (half + 1) * rows_half)
            out_sc[hs, :] = jnp.dot(p_sc[hs, :], vall_sc[band, half * LANES:(half + 1) * LANES],
                                    preferred_element_type=F32)
        for c in range(D_MODEL // LANES):
            oa = out_sc[(2 * c) * WINDOW:(2 * c + 1) * WINDOW, :]
            ob = out_sc[(2 * c + 1) * WINDOW:(2 * c + 2) * WINDOW, :]
            o_sc[blk * WINDOW:(blk + 1) * WINDOW, c * LANES:(c + 1) * LANES] = _gather_head_pair(
                oa, ob, c, lane).astype(BF16)
    y_ref[...] = x + jnp.dot(o_sc[...], wo_ref[...], preferred_element_type=F32)


def _swa_prompt_call(x, g, w_qkv, qn, kn, cos, sin, sinks, w_o, *, name):
    b, length, _ = x.shape
    tq = SWA_TQ
    assert length % tq == 0
    row_spec = pl.BlockSpec((None, tq, D_MODEL), lambda i, t: (i, t, 0))
    tab_spec = pl.BlockSpec((tq, LANES), lambda i, t: (t, 0))
    cache_spec = pl.BlockSpec((None, WINDOW, KV_WIDTH), lambda i, t: (i, 0, 0))
    return pl.pallas_call(
        functools.partial(_swa_prompt_kernel, tq=tq),
        grid=(b, length // tq),
        in_specs=[
            row_spec,
            _const_spec((1, D_MODEL)),
            _const_spec((D_MODEL, QKV_WIDTH)),
            _const_spec((1, LANES)),
            _const_spec((1, LANES)),
            tab_spec,
            tab_spec,
            pl.BlockSpec(memory_space=pltpu.SMEM),
            _const_spec((D_MODEL, D_MODEL)),
        ],
        out_specs=[row_spec, cache_spec, cache_spec],
        out_shape=[
            jax.ShapeDtypeStruct((b, length, D_MODEL), F32),
            jax.ShapeDtypeStruct((b, WINDOW, KV_WIDTH), F32),
            jax.ShapeDtypeStruct((b, WINDOW, KV_WIDTH), F32),
        ],
        scratch_shapes=[
            pltpu.VMEM((tq + WINDOW, KV_WIDTH), BF16),
            pltpu.VMEM((tq + WINDOW, KV_WIDTH), BF16),
            pltpu.VMEM((tq // WINDOW, N_HEADS * WINDOW, LANES), BF16),
            pltpu.VMEM((2, N_HEADS * WINDOW, 2 * WINDOW), F32),
            pltpu.VMEM((N_HEADS * WINDOW, 2 * WINDOW), BF16),
            pltpu.VMEM((N_HEADS * WINDOW, LANES), F32),
            pltpu.VMEM((tq, D_MODEL), BF16),
        ],
        compiler_params=pltpu.CompilerParams(
            dimension_semantics=("arbitrary", "arbitrary"), vmem_limit_bytes=VMEM_LIMIT),
        name=name,
    )(x, g, w_qkv, qn, kn, cos, sin, sinks, w_o)


def _swa_sample_kernel(x_ref, g_ref, wqkv_ref, qn_ref, kn_ref, cos_ref, sin_ref, sink_ref,
                       ck_ref, cv_ref, wo_ref, y_ref, ko_ref, vo_ref,
                       q_sc, knew_sc, vnew_sc, kall_sc, vall_sc, o_sc):
    pad_rows = slice(WINDOW + DEC_SEQ, 2 * WINDOW)

    @pl.when(pl.program_id(0) == 0)
    def _():
        kall_sc[pad_rows, :] = jnp.zeros((WINDOW - DEC_SEQ, KV_WIDTH), F32)
        vall_sc[pad_rows, :] = jnp.zeros((WINDOW - DEC_SEQ, KV_WIDTH), F32)

    x = x_ref[...]
    h = _rms(x, g_ref[...]).astype(BF16)
    qkv = jnp.dot(h, wqkv_ref[...], preferred_element_type=F32)
    lane = lax.broadcasted_iota(jnp.int32, (1, LANES), 1)
    cos = cos_ref[...]
    sin = sin_ref[...]
    scale = HEAD_DIM ** -0.5
    for c in range(D_MODEL // LANES):
        q_sc[:, c * LANES:(c + 1) * LANES] = _qk_norm_rope(
            qkv[:, c * LANES:(c + 1) * LANES], qn_ref[...], cos, sin, lane) * scale
    for c in range(KV_WIDTH // LANES):
        knew_sc[:, c * LANES:(c + 1) * LANES] = _qk_norm_rope(
            qkv[:, D_MODEL + c * LANES:D_MODEL + (c + 1) * LANES], kn_ref[...], cos, sin, lane)
    vnew_sc[...] = qkv[:, D_MODEL + KV_WIDTH:]

    t_idx = lax.broadcasted_iota(jnp.int32, (N_HEADS * DEC_SEQ, 1), 0) & (DEC_SEQ - 1)
    key_idx = lax.broadcasted_iota(jnp.int32, (1, 2 * WINDOW), 1)
    valid = ((key_idx < WINDOW) & (key_idx > t_idx)) | (
        (key_idx >= WINDOW) & (key_idx - WINDOW <= t_idx))
    sink = sink_ref[...]
    rows_half = HALF_HEADS * DEC_SEQ

    def per_batch(b, carry):
        off = pl.multiple_of(b * DEC_SEQ, DEC_SEQ)
        kall_sc[0:WINDOW, :] = ck_ref[b]
        vall_sc[0:WINDOW, :] = cv_ref[b]
        kall_sc[WINDOW:WINDOW + DEC_SEQ, :] = knew_sc[pl.ds(off, DEC_SEQ), :]
        vall_sc[WINDOW:WINDOW + DEC_SEQ, :] = vnew_sc[pl.ds(off, DEC_SEQ), :]
        ko_ref[b] = kall_sc[DEC_SEQ:WINDOW + DEC_SEQ, :]
        vo_ref[b] = vall_sc[DEC_SEQ:WINDOW + DEC_SEQ, :]
        qb = q_sc[pl.ds(off, DEC_SEQ), :]
        pieces = [_place_query_head(qb[:, (hd // 2) * LANES:(hd // 2 + 1) * LANES], hd, lane)
                  for hd in range(N_HEADS)]
        lgs = []
        for half in range(2):
            qe = jnp.concatenate(pieces[half * HALF_HEADS:(half + 1) * HALF_HEADS], axis=0)
            kb = kall_sc[:, half * LANES:(half + 1) * LANES].astype(BF16)
            lgs.append(lax.dot_general(qe.astype(BF16), kb, _NT, preferred_element_type=F32))
        p = _softmax_with_sink(jnp.concatenate(lgs, axis=0), valid, sink)
        outs = []
        for half in range(2):
            vb = vall_sc[:, half * LANES:(half + 1) * LANES].astype(BF16)
            outs.append(jnp.dot(p[half * rows_half:(half + 1) * rows_half, :], vb,
                                preferred_element_type=F32))
        out = jnp.concatenate(outs, axis=0)
        o_blocks = []
        for c in range(D_MODEL // LANES):
            oa = out[(2 * c) * DEC_SEQ:(2 * c + 1) * DEC_SEQ, :]
            ob = out[(2 * c + 1) * DEC_SEQ:(2 * c + 2) * DEC_SEQ, :]
            o_blocks.append(_gather_head_pair(oa, ob, c, lane))
        o_sc[pl.ds(off, DEC_SEQ), :] = jnp.concatenate(o_blocks, axis=1)
        return carry

    lax.fori_loop(0, SWA_TB, per_batch, 0)
    y_ref[...] = x + jnp.dot(o_sc[...].astype(BF16), wo_ref[...], preferred_element_type=F32)


def _swa_sample_call(x, g, w_qkv, qn, kn, cos, sin, sink_col, cache_k, cache_v, w_o, *, name):
    n = x.shape[0]
    tm = SWA_TB * DEC_SEQ
    row_spec = pl.BlockSpec((tm, D_MODEL), lambda i: (i, 0))
    cache_spec = pl.BlockSpec((SWA_TB, WINDOW, KV_WIDTH), lambda i: (i, 0, 0))
    return pl.pallas_call(
        _swa_sample_kernel,
        grid=(n // tm,),
        in_specs=[
            row_spec,
            _const_spec((1, D_MODEL)),
            _const_spec((D_MODEL, QKV_WIDTH)),
            _const_spec((1, LANES)),
            _const_spec((1, LANES)),
            _const_spec((tm, LANES)),
            _const_spec((tm, LANES)),
            _const_spec((N_HEADS * DEC_SEQ, 1)),
            cache_spec,
            cache_spec,
            _const_spec((D_MODEL, D_MODEL)),
        ],
        out_specs=[row_spec, cache_spec, cache_spec],
        out_shape=[
            jax.ShapeDtypeStruct((n, D_MODEL), F32),
            jax.ShapeDtypeStruct((DEC_BATCH, WINDOW, KV_WIDTH), F32),
            jax.ShapeDtypeStruct((DEC_BATCH, WINDOW, KV_WIDTH), F32),
        ],
        scratch_shapes=[
            pltpu.VMEM((tm, D_MODEL), F32),
            pltpu.VMEM((tm, KV_WIDTH), F32),
            pltpu.VMEM((tm, KV_WIDTH), F32),
            pltpu.VMEM((2 * WINDOW, KV_WIDTH), F32),
            pltpu.VMEM((2 * WINDOW, KV_WIDTH), F32),
            pltpu.VMEM((tm, D_MODEL), F32),
        ],
        compiler_params=pltpu.CompilerParams(
            dimension_semantics=("arbitrary",), vmem_limit_bytes=VMEM_LIMIT),
        name=name,
    )(x, g, w_qkv, qn, kn, cos, sin, sink_col, cache_k, cache_v, w_o)


def _rope_tables(pos):
    inv_freq = jnp.power(jnp.float32(ROPE_THETA),
                         -jnp.arange(0, HEAD_DIM, 2, dtype=F32) / HEAD_DIM)
    ang = pos.astype(F32)[:, None] * inv_freq[None, :]
    reps = LANES // (HEAD_DIM // 2)
    cos = jnp.tile(jnp.cos(ang), (1, reps))
    lane = jnp.arange(LANES)
    sign = jnp.where((lane % HEAD_DIM) < HEAD_DIM // 2, -1.0, 1.0).astype(F32)
    sin = jnp.tile(jnp.sin(ang), (1, reps)) * sign[None, :]
    return cos, sin


def kernel(x_prompt, x_sample, cache_swa_k, cache_swa_v, state_ffn_conv, mix_norm_g, sgu_w_in,
           sgu_ln_g, sgu_ln_b, sgu_w_s, sgu_b_s, sgu_w_out, attn_w_qkv, attn_q_norm, attn_k_norm,
           attn_sinks, attn_w_o, ffn_norm_g, ffn_w_up, ffn_conv_w, ffn_conv_b, ffn_w_down):
    n_s = DEC_BATCH * DEC_SEQ

    w_s = jnp.where(jnp.tril(jnp.ones((CHUNK, CHUNK), dtype=bool))[None], sgu_w_s[0], 0.0)
    mix_p = w_s.astype(BF16)
    eye = jnp.eye(CHUNK // DEC_SEQ, dtype=F32)
    mix_s = jnp.einsum('ab,gts->gatbs', eye, w_s[:, :DEC_SEQ, :DEC_SEQ]).reshape(
        SGU_GROUPS, CHUNK, CHUNK).astype(BF16)
    bias_p = jnp.repeat(sgu_b_s[0].T, SGU_GROUP_DIM, axis=1)
    bias_s = jnp.tile(bias_p[:DEC_SEQ], (CHUNK // DEC_SEQ, 1))
    sgu_args = (mix_norm_g[0][None], sgu_w_in[0].astype(BF16), sgu_ln_g[0][None], sgu_ln_b[0][None])
    w_out = sgu_w_out[0].astype(BF16)
    (yp,) = _sgu_call(x_prompt.reshape(BATCH * SEQ, D_MODEL), *sgu_args, mix_p, bias_p, w_out,
                      emit_v=False, name="sgu_prompt")
    ys, v_s = _sgu_call(x_sample.reshape(n_s, D_MODEL), *sgu_args, mix_s, bias_s, w_out,
                        emit_v=True, name="sgu_sample")
    yp = yp.reshape(BATCH, SEQ, D_MODEL)

    conv_p, conv_s = [], []

    def conv_ffn(layer, yp, ys):
        weights = (ffn_norm_g[layer][None], ffn_w_up[layer].astype(BF16), ffn_conv_w[layer],
                   ffn_conv_b[layer][None], ffn_w_down[layer].astype(BF16))
        yp, st_p = _ffn_call(yp, *weights, name=f"ffn_prompt_{layer}")
        st_in = state_ffn_conv[layer].reshape(DEC_BATCH, (CONV_W - 1) * 2 * D_FF)
        ys, st_s = _ffn_call(ys, *weights, st_in, name=f"ffn_sample_{layer}")
        conv_p.append(st_p[:, SUBLANES - (CONV_W - 1):, :])
        conv_s.append(st_s.reshape(DEC_BATCH, CONV_W - 1, 2 * D_FF))
        return yp, ys

    yp, ys = conv_ffn(0, yp, ys)

    w_qkv = attn_w_qkv[0].astype(BF16)
    w_o = attn_w_o[0].astype(BF16)
    qn = jnp.tile(attn_q_norm[0], LANES // HEAD_DIM)[None]
    kn = jnp.tile(attn_k_norm[0], LANES // HEAD_DIM)[None]
    g = mix_norm_g[1][None]
    cos_p, sin_p = _rope_tables(jnp.arange(SEQ, dtype=jnp.int32))
    yp, k_p, v_p = _swa_prompt_call(yp, g, w_qkv, qn, kn, cos_p, sin_p, attn_sinks[0], w_o,
                                    name="swa_prompt")
    pos_s = PAST_LEN + (jnp.arange(SWA_TB * DEC_SEQ, dtype=jnp.int32) % DEC_SEQ)
    cos_s, sin_s = _rope_tables(pos_s)
    sink_col = jnp.repeat(attn_sinks[0], DEC_SEQ)[:, None]
    ys, k_s, v_s_cache = _swa_sample_call(
        ys, g, w_qkv, qn, kn, cos_s, sin_s, sink_col,
        cache_swa_k[0].reshape(DEC_BATCH, WINDOW, KV_WIDTH),
        cache_swa_v[0].reshape(DEC_BATCH, WINDOW, KV_WIDTH), w_o, name="swa_sample")

    yp, ys = conv_ffn(1, yp, ys)

    cache_shape_p = (1, BATCH, WINDOW, N_KV_HEADS, HEAD_DIM)
    cache_shape_s = (1, DEC_BATCH, WINDOW, N_KV_HEADS, HEAD_DIM)
    return (yp,
            ys.reshape(DEC_BATCH, DEC_SEQ, D_MODEL),
            v_s.reshape(1, DEC_BATCH, DEC_SEQ, SGU_WIDTH),
            k_p.reshape(cache_shape_p),
            v_p.reshape(cache_shape_p),
            k_s.reshape(cache_shape_s),
            v_s_cache.reshape(cache_shape_s),
            jnp.stack(conv_p),
            jnp.stack(conv_s))
```

```python
import functools

import jax
import jax.numpy as jnp
from jax import lax
from jax.experimental import pallas as pl
from jax.experimental.pallas import tpu as pltpu

F32 = jnp.float32
BF16 = jnp.bfloat16

D_MODEL = 1024
BATCH = 8
SEQ = 2048
DEC_BATCH = 128
DEC_SEQ = 8
PAST_LEN = 8192
CHUNK = 128
SGU_WIDTH = 2 * D_MODEL
SGU_GROUPS = 8
SGU_GROUP_DIM = SGU_WIDTH // SGU_GROUPS
HEAD_DIM = 64
N_HEADS = D_MODEL // HEAD_DIM
N_KV_HEADS = 4
KV_WIDTH = N_KV_HEADS * HEAD_DIM
QKV_WIDTH = D_MODEL + 2 * KV_WIDTH
WINDOW = 128
ROPE_THETA = 10000.0
D_FF = 2816
CONV_W = 3
EPS = 1e-6

LANES = 128
SUBLANES = 8
FF_CHUNK = 256
N_FF_CHUNKS = D_FF // FF_CHUNK
VMEM_LIMIT = 56 * 1024 * 1024

SGU_TM = 512
SGU_STREAMS = 2
SGU_COLS = 512
FFN_TM = 512
SWA_TQ = 512
SWA_TB = 16


def _const_spec(shape):
    nd = len(shape)
    return pl.BlockSpec(shape, lambda *_: (0,) * nd, pipeline_mode=pl.Buffered(1))


def _layer_spec(shape, layer):
    nd = len(shape)
    return pl.BlockSpec((None,) + shape, lambda *_: (layer,) + (0,) * nd,
                        pipeline_mode=pl.Buffered(1))


def _rms(x, g):
    ms = jnp.mean(x * x, axis=-1, keepdims=True)
    return x * lax.rsqrt(ms + EPS) * g


def _sgu_kernel(x_ref, g_ref, win_ref, lng_ref, lnb_ref, mix_ref, bias_ref, wout_ref,
                *rest, tm, emit_v):
    if emit_v:
        y_ref, v_ref, u_sc, v_sc, vb_sc, p_sc = rest
    else:
        y_ref, u_sc, v_sc, vb_sc, p_sc = rest
        v_ref = None
    cw = SGU_COLS
    n_cols = 2 * SGU_WIDTH // cw
    streams = [slice(s * tm // SGU_STREAMS, (s + 1) * tm // SGU_STREAMS)
               for s in range(SGU_STREAMS)]

    def in_proj(rs):
        h = _rms(x_ref[rs, :], g_ref[...]).astype(BF16)

        def z(c):
            return jnp.dot(h, win_ref[:, c * cw:(c + 1) * cw], preferred_element_type=F32)

        nxt = z(0)
        for c in range(n_cols):
            cur = nxt
            if c + 1 < n_cols:
                nxt = z(c + 1)
            dst = u_sc if c < n_cols // 2 else v_sc
            c0 = (c % (n_cols // 2)) * cw
            dst[rs, c0:c0 + cw] = jax.nn.gelu(cur)

    def layer_norm(rs):
        v = v_sc[rs, :]
        mu = jnp.mean(v, axis=-1, keepdims=True)
        vc = v - mu
        var = jnp.mean(vc * vc, axis=-1, keepdims=True)
        vn = vc * lax.rsqrt(var + EPS) * lng_ref[...] + lnb_ref[...]
        if emit_v:
            v_ref[rs, :] = vn
        vb_sc[rs, :] = vn.astype(BF16)

    def mix(rs):
        for c in range(rs.start // CHUNK, rs.stop // CHUNK):
            cr = slice(c * CHUNK, (c + 1) * CHUNK)
            for g in range(SGU_GROUPS):
                cs = slice(g * SGU_GROUP_DIM, (g + 1) * SGU_GROUP_DIM)
                mixed = jnp.dot(mix_ref[g], vb_sc[cr, cs], preferred_element_type=F32) + bias_ref[:, cs]
                p_sc[cr, cs] = (u_sc[cr, cs] * mixed).astype(BF16)

    def out_proj(rs):
        y_ref[rs, :] = x_ref[rs, :] + jnp.dot(p_sc[rs, :], wout_ref[...], preferred_element_type=F32)

    for rs in streams:
        in_proj(rs)
    for i, rs in enumerate(streams):
        layer_norm(rs)
        mix(rs)
        if i > 0:
            out_proj(streams[i - 1])
    out_proj(streams[-1])


def _sgu_call(x, g, w_in, ln_g, ln_b, mix, bias, w_out, *, emit_v, name):
    n = x.shape[0]
    tm = SGU_TM
    assert n % tm == 0
    row_spec = pl.BlockSpec((tm, D_MODEL), lambda i: (i, 0))
    out_shape = [jax.ShapeDtypeStruct((n, D_MODEL), F32)]
    out_specs = [row_spec]
    if emit_v:
        out_shape.append(jax.ShapeDtypeStruct((n, SGU_WIDTH), F32))
        out_specs.append(pl.BlockSpec((tm, SGU_WIDTH), lambda i: (i, 0)))
    return pl.pallas_call(
        functools.partial(_sgu_kernel, tm=tm, emit_v=emit_v),
        grid=(n // tm,),
        in_specs=[
            row_spec,
            _const_spec((1, D_MODEL)),
            _const_spec((D_MODEL, 2 * SGU_WIDTH)),
            _const_spec((1, SGU_WIDTH)),
            _const_spec((1, SGU_WIDTH)),
            _const_spec((SGU_GROUPS, CHUNK, CHUNK)),
            _const_spec((CHUNK, SGU_WIDTH)),
            _const_spec((SGU_WIDTH, D_MODEL)),
        ],
        out_specs=out_specs,
        out_shape=out_shape,
        scratch_shapes=[
            pltpu.VMEM((tm, SGU_WIDTH), F32),
            pltpu.VMEM((tm, SGU_WIDTH), F32),
            pltpu.VMEM((tm, SGU_WIDTH), BF16),
            pltpu.VMEM((tm, SGU_WIDTH), BF16),
        ],
        compiler_params=pltpu.CompilerParams(
            dimension_semantics=("arbitrary",), vmem_limit_bytes=VMEM_LIMIT),
        name=name,
    )(x, g, w_in, ln_g, ln_b, mix, bias, w_out)


def _ffn_cols(j, part):
    lo = part * D_FF + j * FF_CHUNK
    return slice(lo, lo + FF_CHUNK)


def _ffn_kernel(*refs, tm, sample):
    if sample:
        x_ref, g_ref, wup_ref, cw_ref, cb_ref, wdown_ref, sti_ref, y_ref, st_ref, p_sc = refs
    else:
        x_ref, g_ref, wup_ref, cw_ref, cb_ref, wdown_ref, y_ref, st_ref, carry_sc, p_sc = refs

        @pl.when(pl.program_id(1) == 0)
        def _():
            carry_sc[...] = jnp.zeros_like(carry_sc)

    x = x_ref[...]
    h = _rms(x, g_ref[...]).astype(BF16)

    def up(j):
        return [jnp.dot(h, wup_ref[:, _ffn_cols(j, part)], preferred_element_type=F32)
                for part in range(2)]

    def taps(cs):
        return cw_ref[0:1, cs], cw_ref[1:2, cs], cw_ref[2:3, cs], cb_ref[:, cs]

    def conv_prompt(a, cs):
        w0, w1, w2, cb = taps(cs)
        sub = lax.broadcasted_iota(jnp.int32, (SUBLANES, 1), 0)
        c0 = carry_sc[SUBLANES - 2:SUBLANES - 1, cs]
        c1 = carry_sc[SUBLANES - 1:SUBLANES, cs]
        r1 = pltpu.roll(a, 1, 0)
        r2 = pltpu.roll(a, 2, 0)
        h1 = jnp.where(sub == 0, c1, r1[:SUBLANES])
        h2 = jnp.where(sub == 0, c0, jnp.where(sub == 1, c1, r2[:SUBLANES]))
        p1 = jnp.concatenate([h1, r1[SUBLANES:]], axis=0)
        p2 = jnp.concatenate([h2, r2[SUBLANES:]], axis=0)
        carry_sc[:, cs] = a[tm - SUBLANES:, :]
        return cb + (w0 * p2 + w1 * p1 + w2 * a)

    def conv_sample(a, cs):
        w0, w1, w2, cb = [v[None] for v in taps(cs)]
        cs1 = slice(cs.start + 2 * D_FF, cs.stop + 2 * D_FF)
        a3 = a.reshape(tm // DEC_SEQ, DEC_SEQ, FF_CHUNK)
        t = lax.broadcasted_iota(jnp.int32, (1, DEC_SEQ, 1), 1)
        s0 = jnp.broadcast_to(sti_ref[:, cs][:, None, :], a3.shape)
        s1 = jnp.broadcast_to(sti_ref[:, cs1][:, None, :], a3.shape)
        p1 = jnp.where(t == 0, s1, pltpu.roll(a3, 1, 1))
        p2 = jnp.where(t == 0, s0, jnp.where(t == 1, s1, pltpu.roll(a3, 2, 1)))
        st_ref[:, cs] = a3[:, DEC_SEQ - 2, :]
        st_ref[:, cs1] = a3[:, DEC_SEQ - 1, :]
        return (cb + (w0 * p2 + w1 * p1 + w2 * a3)).reshape(tm, FF_CHUNK)

    conv = conv_sample if sample else conv_prompt
    nxt = up(0)
    for j in range(N_FF_CHUNKS):
        cur = nxt
        if j + 1 < N_FF_CHUNKS:
            nxt = up(j + 1)
        gate = conv(cur[0], _ffn_cols(j, 0))
        val = conv(cur[1], _ffn_cols(j, 1))
        p_sc[:, j * FF_CHUNK:(j + 1) * FF_CHUNK] = (gate * jax.nn.sigmoid(gate) * val).astype(BF16)
    y_ref[...] = x + jnp.dot(p_sc[...], wdown_ref[...], preferred_element_type=F32)
    if not sample:
        st_ref[...] = carry_sc[...]


def _ffn_call(x, layer, g, w_up, conv_w, conv_b, w_down, state=None, *, name):
    sample = state is not None
    tm = FFN_TM
    weight_specs = [
        _layer_spec((1, D_MODEL), layer),
        _layer_spec((D_MODEL, 2 * D_FF), layer),
        _layer_spec((CONV_W, 2 * D_FF), layer),
        _layer_spec((1, 2 * D_FF), layer),
        _layer_spec((D_FF, D_MODEL), layer),
    ]
    scratch = [pltpu.VMEM((tm, D_FF), BF16)]
    if sample:
        assert DEC_SEQ == SUBLANES
        n = x.shape[0]
        assert n % tm == 0
        grid = (n // tm,)
        row_spec = pl.BlockSpec((tm, D_MODEL), lambda i: (i, 0))
        st_spec = pl.BlockSpec((tm // DEC_SEQ, (CONV_W - 1) * 2 * D_FF), lambda i: (i, 0))
        in_specs = [row_spec] + weight_specs + [st_spec]
        st_shape = jax.ShapeDtypeStruct(state.shape, F32)
        args = (x, g, w_up, conv_w, conv_b, w_down, state)
    else:
        b, length, _ = x.shape
        assert length % tm == 0
        grid = (b, length // tm)
        row_spec = pl.BlockSpec((None, tm, D_MODEL), lambda i, t: (i, t, 0))
        st_spec = pl.BlockSpec((None, SUBLANES, 2 * D_FF), lambda i, t: (i, 0, 0))
        in_specs = [row_spec] + weight_specs
        st_shape = jax.ShapeDtypeStruct((b, SUBLANES, 2 * D_FF), F32)
        scratch = [pltpu.VMEM((SUBLANES, 2 * D_FF), F32)] + scratch
        args = (x, g, w_up, conv_w, conv_b, w_down)
    return pl.pallas_call(
        functools.partial(_ffn_kernel, tm=tm, sample=sample),
        grid=grid,
        in_specs=in_specs,
        out_specs=[row_spec, st_spec],
        out_shape=[jax.ShapeDtypeStruct(x.shape, F32), st_shape],
        scratch_shapes=scratch,
        compiler_params=pltpu.CompilerParams(
            dimension_semantics=("arbitrary",) * len(grid), vmem_limit_bytes=VMEM_LIMIT),
        name=name,
    )(*args)


def _qk_norm_rope(xb, gain, cos, sin, lane):
    sq = xb * xb
    lo = jnp.sum(jnp.where(lane < HEAD_DIM, sq, 0.0), axis=-1, keepdims=True)
    hi = jnp.sum(jnp.where(lane >= HEAD_DIM, sq, 0.0), axis=-1, keepdims=True)
    ms = jnp.where(lane < HEAD_DIM, lo, hi) * (1.0 / HEAD_DIM)
    y = xb * lax.rsqrt(ms + EPS) * gain
    first_half = (lane & (HEAD_DIM - 1)) < HEAD_DIM // 2
    rot = jnp.where(first_half, pltpu.roll(y, LANES - HEAD_DIM // 2, 1),
                    pltpu.roll(y, HEAD_DIM // 2, 1))
    return y * cos + rot * sin


def _place_query_head(qb, hd, lane):
    kv_half = (hd // (N_HEADS // N_KV_HEADS)) % 2
    src = qb if hd % 2 == kv_half else pltpu.roll(qb, HEAD_DIM, 1)
    keep = (lane >= HEAD_DIM) if kv_half == 1 else (lane < HEAD_DIM)
    return jnp.where(keep, src, 0.0)


def _gather_head_pair(out_a, out_b, c, lane):
    kv_half = (c // 2) % 2
    if kv_half == 0:
        return jnp.where(lane < HEAD_DIM, out_a, pltpu.roll(out_b, HEAD_DIM, 1))
    return jnp.where(lane < HEAD_DIM, pltpu.roll(out_a, HEAD_DIM, 1), out_b)


def _softmax_with_sink(lg, valid, sink):
    lg = jnp.where(valid, lg, -jnp.inf)
    m = jnp.maximum(jnp.max(lg, axis=-1, keepdims=True), sink)
    p = jnp.exp(lg - m)
    den = jnp.sum(p, axis=-1, keepdims=True) + jnp.exp(sink - m)
    return (p / den).astype(BF16)


_NT = (((1,), (1,)), ((), ()))
HALF_HEADS = N_HEADS // 2


def _swa_prompt_kernel(x_ref, g_ref, wqkv_ref, qn_ref, kn_ref, cos_ref, sin_ref, sink_ref,
                       wo_ref, y_ref, kc_ref, vc_ref,
                       kall_sc, vall_sc, qexp_sc, lg_sc, p_sc, out_sc, o_sc, *, tq):
    n = pl.program_id(1)
    nb = tq // WINDOW
    prev = slice(0, WINDOW)
    last = slice(nb * WINDOW, (nb + 1) * WINDOW)

    @pl.when(n == 0)
    def _():
        kall_sc[prev, :] = jnp.zeros((WINDOW, KV_WIDTH), BF16)
        vall_sc[prev, :] = jnp.zeros((WINDOW, KV_WIDTH), BF16)

    @pl.when(n > 0)
    def _():
        kall_sc[prev, :] = kall_sc[last, :]
        vall_sc[prev, :] = vall_sc[last, :]

    x = x_ref[...]
    h = _rms(x, g_ref[...]).astype(BF16)
    qkv = jnp.dot(h, wqkv_ref[...], preferred_element_type=F32)
    lane = lax.broadcasted_iota(jnp.int32, (1, LANES), 1)
    cos = cos_ref[...]
    sin = sin_ref[...]
    scale = HEAD_DIM ** -0.5

    k_new = jnp.concatenate(
        [_qk_norm_rope(qkv[:, D_MODEL + c * LANES:D_MODEL + (c + 1) * LANES],
                       kn_ref[...], cos, sin, lane) for c in range(KV_WIDTH // LANES)], axis=1)
    v_new = qkv[:, D_MODEL + KV_WIDTH:]
    kall_sc[WINDOW:, :] = k_new.astype(BF16)
    vall_sc[WINDOW:, :] = v_new.astype(BF16)
    kc_ref[...] = k_new[tq - WINDOW:, :]
    vc_ref[...] = v_new[tq - WINDOW:, :]
    for c in range(D_MODEL // LANES):
        qb = _qk_norm_rope(qkv[:, c * LANES:(c + 1) * LANES], qn_ref[...], cos, sin, lane) * scale
        for hd in (2 * c, 2 * c + 1):
            placed = _place_query_head(qb, hd, lane).astype(BF16)
            for blk in range(nb):
                qexp_sc[blk, hd * WINDOW:(hd + 1) * WINDOW, :] = placed[blk * WINDOW:(blk + 1) * WINDOW, :]

    tq_idx = lax.broadcasted_iota(jnp.int32, (WINDOW, 1), 0)
    key_idx = lax.broadcasted_iota(jnp.int32, (1, 2 * WINDOW), 1)
    in_window = (key_idx > tq_idx) & (key_idx <= tq_idx + WINDOW)
    rows_half = HALF_HEADS * WINDOW

    def logits(blk):
        band = slice(blk * WINDOW, (blk + 2) * WINDOW)
        for half in range(2):
            hs = slice(half * rows_half, (half + 1) * rows_half)
            lg_sc[blk % 2, hs, :] = lax.dot_general(
                qexp_sc[blk, hs, :], kall_sc[band, half * LANES:(half + 1) * LANES], _NT,
                preferred_element_type=F32)

    logits(0)
    for blk in range(nb):
        if blk + 1 < nb:
            logits(blk + 1)
        first_key = jnp.where(n * nb + blk > 0, 0, WINDOW)
        valid = in_window & (key_idx >= first_key)
        for hd in range(N_HEADS):
            hr = slice(hd * WINDOW, (hd + 1) * WINDOW)
            p_sc[hr, :] = _softmax_with_sink(lg_sc[blk % 2, hr, :], valid, sink_ref[hd])
        band = slice(blk * WINDOW, (blk + 2) * WINDOW)
        for half in range(2):
            hs = slice(half * rows_half, (half + 1) * rows_half)
            out_sc[hs, :] = jnp.dot(p_sc[hs, :], vall_sc[band, half * LANES:(half + 1) * LANES],
                                    preferred_element_type=F32)
        for c in range(D_MODEL // LANES):
            oa = out_sc[(2 * c) * WINDOW:(2 * c + 1) * WINDOW, :]
            ob = out_sc[(2 * c + 1) * WINDOW:(2 * c + 2) * WINDOW, :]
            o_sc[blk * WINDOW:(blk + 1) * WINDOW, c * LANES:(c + 1) * LANES] = _gather_head_pair(
                oa, ob, c, lane).astype(BF16)
    y_ref[...] = x + jnp.dot(o_sc[...], wo_ref[...], preferred_element_type=F32)


def _swa_prompt_call(x, g, w_qkv, qn, kn, cos, sin, sinks, w_o, *, name):
    b, length, _ = x.shape
    tq = SWA_TQ
    assert length % tq == 0
    row_spec = pl.BlockSpec((None, tq, D_MODEL), lambda i, t: (i, t, 0))
    tab_spec = pl.BlockSpec((tq, LANES), lambda i, t: (t, 0))
    cache_spec = pl.BlockSpec((None, WINDOW, KV_WIDTH), lambda i, t: (i, 0, 0))
    return pl.pallas_call(
        functools.partial(_swa_prompt_kernel, tq=tq),
        grid=(b, length // tq),
        in_specs=[
            row_spec,
            _const_spec((1, D_MODEL)),
            _const_spec((D_MODEL, QKV_WIDTH)),
            _const_spec((1, LANES)),
            _const_spec((1, LANES)),
            tab_spec,
            tab_spec,
            pl.BlockSpec(memory_space=pltpu.SMEM),
            _const_spec((D_MODEL, D_MODEL)),
        ],
        out_specs=[row_spec, cache_spec, cache_spec],
        out_shape=[
            jax.ShapeDtypeStruct((b, length, D_MODEL), F32),
            jax.ShapeDtypeStruct((b, WINDOW, KV_WIDTH), F32),
            jax.ShapeDtypeStruct((b, WINDOW, KV_WIDTH), F32),
        ],
        scratch_shapes=[
            pltpu.VMEM((tq + WINDOW, KV_WIDTH), BF16),
            pltpu.VMEM((tq + WINDOW, KV_WIDTH), BF16),
            pltpu.VMEM((tq // WINDOW, N_HEADS * WINDOW, LANES), BF16),
            pltpu.VMEM((2, N_HEADS * WINDOW, 2 * WINDOW), F32),
            pltpu.VMEM((N_HEADS * WINDOW, 2 * WINDOW), BF16),
            pltpu.VMEM((N_HEADS * WINDOW, LANES), F32),
            pltpu.VMEM((tq, D_MODEL), BF16),
        ],
        compiler_params=pltpu.CompilerParams(
            dimension_semantics=("arbitrary", "arbitrary"), vmem_limit_bytes=VMEM_LIMIT),
        name=name,
    )(x, g, w_qkv, qn, kn, cos, sin, sinks, w_o)


def _swa_sample_kernel(x_ref, g_ref, wqkv_ref, qn_ref, kn_ref, cos_ref, sin_ref, sink_ref,
                       ck_ref, cv_ref, wo_ref, y_ref, ko_ref, vo_ref,
                       q_sc, knew_sc, vnew_sc, o_sc):
    x = x_ref[...]
    h = _rms(x, g_ref[...]).astype(BF16)
    qkv = jnp.dot(h, wqkv_ref[...], preferred_element_type=F32)
    lane = lax.broadcasted_iota(jnp.int32, (1, LANES), 1)
    cos = cos_ref[...]
    sin = sin_ref[...]
    scale = HEAD_DIM ** -0.5
    for c in range(D_MODEL // LANES):
        q_sc[:, c * LANES:(c + 1) * LANES] = _qk_norm_rope(
            qkv[:, c * LANES:(c + 1) * LANES], qn_ref[...], cos, sin, lane) * scale
    for c in range(KV_WIDTH // LANES):
        knew_sc[:, c * LANES:(c + 1) * LANES] = _qk_norm_rope(
            qkv[:, D_MODEL + c * LANES:D_MODEL + (c + 1) * LANES], kn_ref[...], cos, sin, lane)
    vnew_sc[...] = qkv[:, D_MODEL + KV_WIDTH:]

    t_idx = lax.broadcasted_iota(jnp.int32, (N_HEADS * DEC_SEQ, 1), 0) & (DEC_SEQ - 1)
    key_idx = lax.broadcasted_iota(jnp.int32, (1, 2 * WINDOW), 1)
    valid = ((key_idx < WINDOW) & (key_idx > t_idx)) | (
        (key_idx >= WINDOW) & (key_idx - WINDOW <= t_idx))
    sink = sink_ref[...]
    rows_half = HALF_HEADS * DEC_SEQ

    pad = jnp.zeros((WINDOW - DEC_SEQ, KV_WIDTH), F32)

    def rows(b):
        return slice(b * DEC_SEQ, (b + 1) * DEC_SEQ)

    lgs = []
    for b in range(SWA_TB):
        k_old = ck_ref[b]
        k_b = knew_sc[rows(b), :]
        ko_ref[b] = jnp.concatenate([k_old[DEC_SEQ:], k_b], axis=0)
        k_all = jnp.concatenate([k_old, k_b, pad], axis=0).astype(BF16)
        qb = q_sc[rows(b), :]
        pieces = [_place_query_head(qb[:, (hd // 2) * LANES:(hd // 2 + 1) * LANES], hd, lane)
                  for hd in range(N_HEADS)]
        halves = []
        for half in range(2):
            qe = jnp.concatenate(pieces[half * HALF_HEADS:(half + 1) * HALF_HEADS], axis=0)
            halves.append(lax.dot_general(qe.astype(BF16), k_all[:, half * LANES:(half + 1) * LANES],
                                          _NT, preferred_element_type=F32))
        lgs.append(jnp.concatenate(halves, axis=0))
    ps = [_softmax_with_sink(lg, valid, sink) for lg in lgs]
    for b in range(SWA_TB):
        v_old = cv_ref[b]
        v_b = vnew_sc[rows(b), :]
        vo_ref[b] = jnp.concatenate([v_old[DEC_SEQ:], v_b], axis=0)
        v_all = jnp.concatenate([v_old, v_b, pad], axis=0).astype(BF16)
        out = jnp.concatenate(
            [jnp.dot(ps[b][half * rows_half:(half + 1) * rows_half, :],
                     v_all[:, half * LANES:(half + 1) * LANES], preferred_element_type=F32)
             for half in range(2)], axis=0)
        o_blocks = []
        for c in range(D_MODEL // LANES):
            oa = out[(2 * c) * DEC_SEQ:(2 * c + 1) * DEC_SEQ, :]
            ob = out[(2 * c + 1) * DEC_SEQ:(2 * c + 2) * DEC_SEQ, :]
            o_blocks.append(_gather_head_pair(oa, ob, c, lane))
        o_sc[rows(b), :] = jnp.concatenate(o_blocks, axis=1)
    y_ref[...] = x + jnp.dot(o_sc[...].astype(BF16), wo_ref[...], preferred_element_type=F32)


def _swa_sample_call(x, g, w_qkv, qn, kn, cos, sin, sink_col, cache_k, cache_v, w_o, *, name):
    n = x.shape[0]
    tm = SWA_TB * DEC_SEQ
    row_spec = pl.BlockSpec((tm, D_MODEL), lambda i: (i, 0))
    cache_spec = pl.BlockSpec((SWA_TB, WINDOW, KV_WIDTH), lambda i: (i, 0, 0))
    return pl.pallas_call(
        _swa_sample_kernel,
        grid=(n // tm,),
        in_specs=[
            row_spec,
            _const_spec((1, D_MODEL)),
            _const_spec((D_MODEL, QKV_WIDTH)),
            _const_spec((1, LANES)),
            _const_spec((1, LANES)),
            _const_spec((tm, LANES)),
            _const_spec((tm, LANES)),
            _const_spec((N_HEADS * DEC_SEQ, 1)),
            cache_spec,
            cache_spec,
            _const_spec((D_MODEL, D_MODEL)),
        ],
        out_specs=[row_spec, cache_spec, cache_spec],
        out_shape=[
            jax.ShapeDtypeStruct((n, D_MODEL), F32),
            jax.ShapeDtypeStruct((DEC_BATCH, WINDOW, KV_WIDTH), F32),
            jax.ShapeDtypeStruct((DEC_BATCH, WINDOW, KV_WIDTH), F32),
        ],
        scratch_shapes=[
            pltpu.VMEM((tm, D_MODEL), F32),
            pltpu.VMEM((tm, KV_WIDTH), F32),
            pltpu.VMEM((tm, KV_WIDTH), F32),
            pltpu.VMEM((tm, D_MODEL), F32),
        ],
        compiler_params=pltpu.CompilerParams(
            dimension_semantics=("arbitrary",), vmem_limit_bytes=VMEM_LIMIT),
        name=name,
    )(x, g, w_qkv, qn, kn, cos, sin, sink_col, cache_k, cache_v, w_o)


def _rope_tables(pos):
    inv_freq = jnp.power(jnp.float32(ROPE_THETA),
                         -jnp.arange(0, HEAD_DIM, 2, dtype=F32) / HEAD_DIM)
    ang = pos.astype(F32)[:, None] * inv_freq[None, :]
    reps = LANES // (HEAD_DIM // 2)
    cos = jnp.tile(jnp.cos(ang), (1, reps))
    lane = jnp.arange(LANES)
    sign = jnp.where((lane % HEAD_DIM) < HEAD_DIM // 2, -1.0, 1.0).astype(F32)
    sin = jnp.tile(jnp.sin(ang), (1, reps)) * sign[None, :]
    return cos, sin


def kernel(x_prompt, x_sample, cache_swa_k, cache_swa_v, state_ffn_conv, mix_norm_g, sgu_w_in,
           sgu_ln_g, sgu_ln_b, sgu_w_s, sgu_b_s, sgu_w_out, attn_w_qkv, attn_q_norm, attn_k_norm,
           attn_sinks, attn_w_o, ffn_norm_g, ffn_w_up, ffn_conv_w, ffn_conv_b, ffn_w_down):
    n_s = DEC_BATCH * DEC_SEQ

    w_s = jnp.where(jnp.tril(jnp.ones((CHUNK, CHUNK), dtype=bool))[None], sgu_w_s[0], 0.0)
    mix_p = w_s.astype(BF16)
    eye = jnp.eye(CHUNK // DEC_SEQ, dtype=F32)
    mix_s = jnp.einsum('ab,gts->gatbs', eye, w_s[:, :DEC_SEQ, :DEC_SEQ]).reshape(
        SGU_GROUPS, CHUNK, CHUNK).astype(BF16)
    bias_p = jnp.repeat(sgu_b_s[0].T, SGU_GROUP_DIM, axis=1)
    bias_s = jnp.tile(bias_p[:DEC_SEQ], (CHUNK // DEC_SEQ, 1))
    sgu_args = (mix_norm_g[0][None], sgu_w_in[0].astype(BF16), sgu_ln_g[0][None], sgu_ln_b[0][None])
    w_out = sgu_w_out[0].astype(BF16)
    (yp,) = _sgu_call(x_prompt.reshape(BATCH * SEQ, D_MODEL), *sgu_args, mix_p, bias_p, w_out,
                      emit_v=False, name="sgu_prompt")
    ys, v_s = _sgu_call(x_sample.reshape(n_s, D_MODEL), *sgu_args, mix_s, bias_s, w_out,
                        emit_v=True, name="sgu_sample")
    yp = yp.reshape(BATCH, SEQ, D_MODEL)

    conv_p, conv_s = [], []
    ffn_weights = (ffn_norm_g[:, None, :], ffn_w_up.astype(BF16), ffn_conv_w,
                   ffn_conv_b[:, None, :], ffn_w_down.astype(BF16))

    def conv_ffn(layer, yp, ys):
        yp, st_p = _ffn_call(yp, layer, *ffn_weights, name=f"ffn_prompt_{layer}")
        st_in = state_ffn_conv[layer].reshape(DEC_BATCH, (CONV_W - 1) * 2 * D_FF)
        ys, st_s = _ffn_call(ys, layer, *ffn_weights, st_in, name=f"ffn_sample_{layer}")
        conv_p.append(st_p[:, SUBLANES - (CONV_W - 1):, :])
        conv_s.append(st_s.reshape(DEC_BATCH, CONV_W - 1, 2 * D_FF))
        return yp, ys

    yp, ys = conv_ffn(0, yp, ys)

    w_qkv = attn_w_qkv[0].astype(BF16)
    w_o = attn_w_o[0].astype(BF16)
    qn = jnp.tile(attn_q_norm[0], LANES // HEAD_DIM)[None]
    kn = jnp.tile(attn_k_norm[0], LANES // HEAD_DIM)[None]
    g = mix_norm_g[1][None]
    cos_p, sin_p = _rope_tables(jnp.arange(SEQ, dtype=jnp.int32))
    yp, k_p, v_p = _swa_prompt_call(yp, g, w_qkv, qn, kn, cos_p, sin_p, attn_sinks[0], w_o,
                                    name="swa_prompt")
    pos_s = PAST_LEN + (jnp.arange(SWA_TB * DEC_SEQ, dtype=jnp.int32) % DEC_SEQ)
    cos_s, sin_s = _rope_tables(pos_s)
    sink_col = jnp.repeat(attn_sinks[0], DEC_SEQ)[:, None]
    ys, k_s, v_s_cache = _swa_sample_call(
        ys, g, w_qkv, qn, kn, cos_s, sin_s, sink_col,
        cache_swa_k[0].reshape(DEC_BATCH, WINDOW, KV_WIDTH),
        cache_swa_v[0].reshape(DEC_BATCH, WINDOW, KV_WIDTH), w_o, name="swa_sample")

    yp, ys = conv_ffn(1, yp, ys)

    cache_shape_p = (1, BATCH, WINDOW, N_KV_HEADS, HEAD_DIM)
    cache_shape_s = (1, DEC_BATCH, WINDOW, N_KV_HEADS, HEAD_DIM)
    return (yp,
            ys.reshape(DEC_BATCH, DEC_SEQ, D_MODEL),
            v_s.reshape(1, DEC_BATCH, DEC_SEQ, SGU_WIDTH),
            k_p.reshape(cache_shape_p),
            v_p.reshape(cache_shape_p),
            k_s.reshape(cache_shape_s),
            v_s_cache.reshape(cache_shape_s),
            jnp.stack(conv_p),
            jnp.stack(conv_s))
```

```python
import functools

import jax
import jax.numpy as jnp
from jax import lax
from jax.experimental import pallas as pl
from jax.experimental.pallas import tpu as pltpu

F32 = jnp.float32
BF16 = jnp.bfloat16

D_MODEL = 1024
BATCH = 8
SEQ = 2048
DEC_BATCH = 128
DEC_SEQ = 8
PAST_LEN = 8192
CHUNK = 128
SGU_WIDTH = 2 * D_MODEL
SGU_GROUPS = 8
SGU_GROUP_DIM = SGU_WIDTH // SGU_GROUPS
HEAD_DIM = 64
N_HEADS = D_MODEL // HEAD_DIM
N_KV_HEADS = 4
KV_WIDTH = N_KV_HEADS * HEAD_DIM
QKV_WIDTH = D_MODEL + 2 * KV_WIDTH
WINDOW = 128
ROPE_THETA = 10000.0
D_FF = 2816
CONV_W = 3
EPS = 1e-6
LOG2E = 1.4426950408889634

LANES = 128
SUBLANES = 8
FF_CHUNK = 256
N_FF_CHUNKS = D_FF // FF_CHUNK
VMEM_LIMIT = 56 * 1024 * 1024

SGU_TM = 512
SGU_STREAMS = 2
SGU_COLS = 512
FFN_TM = 512
SWA_TQ = 512
SWA_TB = 16


def _const_spec(shape):
    nd = len(shape)
    return pl.BlockSpec(shape, lambda *_: (0,) * nd, pipeline_mode=pl.Buffered(1))


def _layer_spec(shape, layer):
    nd = len(shape)
    return pl.BlockSpec((None,) + shape, lambda *_: (layer,) + (0,) * nd,
                        pipeline_mode=pl.Buffered(1))


def _rms(x, g):
    ms = jnp.mean(x * x, axis=-1, keepdims=True)
    return x * lax.rsqrt(ms + EPS) * g


def _sgu_kernel(x_ref, g_ref, win_ref, lng_ref, lnb_ref, mix_ref, bias_ref, wout_ref,
                *rest, tm, emit_v):
    if emit_v:
        y_ref, v_ref, u_sc, v_sc, vb_sc, p_sc = rest
    else:
        y_ref, u_sc, v_sc, vb_sc, p_sc = rest
        v_ref = None
    cw = SGU_COLS
    n_cols = 2 * SGU_WIDTH // cw
    streams = [slice(s * tm // SGU_STREAMS, (s + 1) * tm // SGU_STREAMS)
               for s in range(SGU_STREAMS)]

    def in_proj(rs):
        h = _rms(x_ref[rs, :], g_ref[...]).astype(BF16)

        def z(c):
            return jnp.dot(h, win_ref[:, c * cw:(c + 1) * cw], preferred_element_type=F32)

        nxt = z(0)
        for c in range(n_cols):
            cur = nxt
            if c + 1 < n_cols:
                nxt = z(c + 1)
            dst = u_sc if c < n_cols // 2 else v_sc
            c0 = (c % (n_cols // 2)) * cw
            dst[rs, c0:c0 + cw] = jax.nn.gelu(cur)

    def layer_norm(rs):
        v = v_sc[rs, :]
        mu = jnp.mean(v, axis=-1, keepdims=True)
        vc = v - mu
        var = jnp.mean(vc * vc, axis=-1, keepdims=True)
        vn = vc * lax.rsqrt(var + EPS) * lng_ref[...] + lnb_ref[...]
        if emit_v:
            v_ref[rs, :] = vn
        vb_sc[rs, :] = vn.astype(BF16)

    def mix(rs):
        for c in range(rs.start // CHUNK, rs.stop // CHUNK):
            cr = slice(c * CHUNK, (c + 1) * CHUNK)
            for g in range(SGU_GROUPS):
                cs = slice(g * SGU_GROUP_DIM, (g + 1) * SGU_GROUP_DIM)
                mixed = jnp.dot(mix_ref[g], vb_sc[cr, cs], preferred_element_type=F32) + bias_ref[:, cs]
                p_sc[cr, cs] = (u_sc[cr, cs] * mixed).astype(BF16)

    def out_proj(rs):
        y_ref[rs, :] = x_ref[rs, :] + jnp.dot(p_sc[rs, :], wout_ref[...], preferred_element_type=F32)

    for rs in streams:
        in_proj(rs)
    for i, rs in enumerate(streams):
        layer_norm(rs)
        mix(rs)
        if i > 0:
            out_proj(streams[i - 1])
    out_proj(streams[-1])


def _sgu_call(x, g, w_in, ln_g, ln_b, mix, bias, w_out, *, emit_v, name):
    n = x.shape[0]
    tm = SGU_TM
    assert n % tm == 0
    row_spec = pl.BlockSpec((tm, D_MODEL), lambda i: (i, 0))
    out_shape = [jax.ShapeDtypeStruct((n, D_MODEL), F32)]
    out_specs = [row_spec]
    if emit_v:
        out_shape.append(jax.ShapeDtypeStruct((n, SGU_WIDTH), F32))
        out_specs.append(pl.BlockSpec((tm, SGU_WIDTH), lambda i: (i, 0)))
    return pl.pallas_call(
        functools.partial(_sgu_kernel, tm=tm, emit_v=emit_v),
        grid=(n // tm,),
        in_specs=[
            row_spec,
            _const_spec((1, D_MODEL)),
            _const_spec((D_MODEL, 2 * SGU_WIDTH)),
            _const_spec((1, SGU_WIDTH)),
            _const_spec((1, SGU_WIDTH)),
            _const_spec((SGU_GROUPS, CHUNK, CHUNK)),
            _const_spec((CHUNK, SGU_WIDTH)),
            _const_spec((SGU_WIDTH, D_MODEL)),
        ],
        out_specs=out_specs,
        out_shape=out_shape,
        scratch_shapes=[
            pltpu.VMEM((tm, SGU_WIDTH), F32),
            pltpu.VMEM((tm, SGU_WIDTH), F32),
            pltpu.VMEM((tm, SGU_WIDTH), BF16),
            pltpu.VMEM((tm, SGU_WIDTH), BF16),
        ],
        compiler_params=pltpu.CompilerParams(
            dimension_semantics=("arbitrary",), vmem_limit_bytes=VMEM_LIMIT),
        name=name,
    )(x, g, w_in, ln_g, ln_b, mix, bias, w_out)


def _ffn_cols(j, part):
    lo = part * D_FF + j * FF_CHUNK
    return slice(lo, lo + FF_CHUNK)


def _ffn_kernel(*refs, tm, sample):
    if sample:
        x_ref, g_ref, wup_ref, cw_ref, cb_ref, wdown_ref, sti_ref, y_ref, st_ref, p_sc = refs
    else:
        x_ref, g_ref, wup_ref, cw_ref, cb_ref, wdown_ref, y_ref, st_ref, carry_sc, p_sc = refs

        @pl.when(pl.program_id(1) == 0)
        def _():
            carry_sc[...] = jnp.zeros_like(carry_sc)

    x = x_ref[...]
    h = _rms(x, g_ref[...]).astype(BF16)

    def up(j):
        return [jnp.dot(h, wup_ref[:, _ffn_cols(j, part)], preferred_element_type=F32)
                for part in range(2)]

    def taps(cs):
        return cw_ref[0:1, cs], cw_ref[1:2, cs], cw_ref[2:3, cs], cb_ref[:, cs]

    def conv_prompt(a, cs):
        w0, w1, w2, cb = taps(cs)
        sub = lax.broadcasted_iota(jnp.int32, (SUBLANES, 1), 0)
        c0 = carry_sc[SUBLANES - 2:SUBLANES - 1, cs]
        c1 = carry_sc[SUBLANES - 1:SUBLANES, cs]
        r1 = pltpu.roll(a, 1, 0)
        r2 = pltpu.roll(a, 2, 0)
        h1 = jnp.where(sub == 0, c1, r1[:SUBLANES])
        h2 = jnp.where(sub == 0, c0, jnp.where(sub == 1, c1, r2[:SUBLANES]))
        p1 = jnp.concatenate([h1, r1[SUBLANES:]], axis=0)
        p2 = jnp.concatenate([h2, r2[SUBLANES:]], axis=0)
        carry_sc[:, cs] = a[tm - SUBLANES:, :]
        return cb + (w0 * p2 + w1 * p1 + w2 * a)

    def conv_sample(a, cs):
        w0, w1, w2, cb = [v[None] for v in taps(cs)]
        cs1 = slice(cs.start + 2 * D_FF, cs.stop + 2 * D_FF)
        a3 = a.reshape(tm // DEC_SEQ, DEC_SEQ, FF_CHUNK)
        t = lax.broadcasted_iota(jnp.int32, (1, DEC_SEQ, 1), 1)
        s0 = jnp.broadcast_to(sti_ref[:, cs][:, None, :], a3.shape)
        s1 = jnp.broadcast_to(sti_ref[:, cs1][:, None, :], a3.shape)
        p1 = jnp.where(t == 0, s1, pltpu.roll(a3, 1, 1))
        p2 = jnp.where(t == 0, s0, jnp.where(t == 1, s1, pltpu.roll(a3, 2, 1)))
        st_ref[:, cs] = a3[:, DEC_SEQ - 2, :]
        st_ref[:, cs1] = a3[:, DEC_SEQ - 1, :]
        return (cb + (w0 * p2 + w1 * p1 + w2 * a3)).reshape(tm, FF_CHUNK)

    conv = conv_sample if sample else conv_prompt
    nxt = up(0)
    for j in range(N_FF_CHUNKS):
        cur = nxt
        if j + 1 < N_FF_CHUNKS:
            nxt = up(j + 1)
        gate = conv(cur[0], _ffn_cols(j, 0))
        val = conv(cur[1], _ffn_cols(j, 1))
        p_sc[:, j * FF_CHUNK:(j + 1) * FF_CHUNK] = (gate * jax.nn.sigmoid(gate) * val).astype(BF16)
    y_ref[...] = x + jnp.dot(p_sc[...], wdown_ref[...], preferred_element_type=F32)
    if not sample:
        st_ref[...] = carry_sc[...]


def _ffn_call(x, layer, g, w_up, conv_w, conv_b, w_down, state=None, *, name):
    sample = state is not None
    tm = FFN_TM
    weight_specs = [
        _layer_spec((1, D_MODEL), layer),
        _layer_spec((D_MODEL, 2 * D_FF), layer),
        _layer_spec((CONV_W, 2 * D_FF), layer),
        _layer_spec((1, 2 * D_FF), layer),
        _layer_spec((D_FF, D_MODEL), layer),
    ]
    scratch = [pltpu.VMEM((tm, D_FF), BF16)]
    if sample:
        assert DEC_SEQ == SUBLANES
        n = x.shape[0]
        assert n % tm == 0
        grid = (n // tm,)
        row_spec = pl.BlockSpec((tm, D_MODEL), lambda i: (i, 0))
        st_spec = pl.BlockSpec((tm // DEC_SEQ, (CONV_W - 1) * 2 * D_FF), lambda i: (i, 0))
        in_specs = [row_spec] + weight_specs + [st_spec]
        st_shape = jax.ShapeDtypeStruct(state.shape, F32)
        args = (x, g, w_up, conv_w, conv_b, w_down, state)
    else:
        b, length, _ = x.shape
        assert length % tm == 0
        grid = (b, length // tm)
        row_spec = pl.BlockSpec((None, tm, D_MODEL), lambda i, t: (i, t, 0))
        st_spec = pl.BlockSpec((None, SUBLANES, 2 * D_FF), lambda i, t: (i, 0, 0))
        in_specs = [row_spec] + weight_specs
        st_shape = jax.ShapeDtypeStruct((b, SUBLANES, 2 * D_FF), F32)
        scratch = [pltpu.VMEM((SUBLANES, 2 * D_FF), F32)] + scratch
        args = (x, g, w_up, conv_w, conv_b, w_down)
    return pl.pallas_call(
        functools.partial(_ffn_kernel, tm=tm, sample=sample),
        grid=grid,
        in_specs=in_specs,
        out_specs=[row_spec, st_spec],
        out_shape=[jax.ShapeDtypeStruct(x.shape, F32), st_shape],
        scratch_shapes=scratch,
        compiler_params=pltpu.CompilerParams(
            dimension_semantics=("arbitrary",) * len(grid), vmem_limit_bytes=VMEM_LIMIT),
        name=name,
    )(*args)


HEAD_ORDER = (0, 4, 1, 5, 2, 6, 3, 7, 8, 12, 9, 13, 10, 14, 11, 15)
ROT_OFFSET = QKV_WIDTH
QKV_EXT_WIDTH = QKV_WIDTH + D_MODEL + KV_WIDTH


def _head_norm_rope(x, xr, a_tab, b_tab, lane):
    sq = x * x
    lo = jnp.sum(jnp.where(lane < HEAD_DIM, sq, 0.0), axis=-1, keepdims=True)
    hi = jnp.sum(jnp.where(lane >= HEAD_DIM, sq, 0.0), axis=-1, keepdims=True)
    ms = jnp.where(lane < HEAD_DIM, lo, hi) * (1.0 / HEAD_DIM)
    return lax.rsqrt(ms + EPS) * (x * a_tab + xr * b_tab)


def _place_query_head(qb, p, lane):
    keep = (lane >= HEAD_DIM) if p % 2 == 1 else (lane < HEAD_DIM)
    return jnp.where(keep, qb, 0.0)


def _gather_head_pair(out_a, out_b, lane):
    return jnp.where(lane < HEAD_DIM, out_a, out_b)


def _softmax_with_sink(lg, valid, sink):
    lg = jnp.where(valid, lg, -jnp.inf)
    m = jnp.maximum(jnp.max(lg, axis=-1, keepdims=True), sink)
    p = jnp.exp2(lg - m)
    den = jnp.sum(p, axis=-1, keepdims=True) + jnp.exp2(sink - m)
    return (p / den).astype(BF16)


_NT = (((1,), (1,)), ((), ()))
HALF_HEADS = N_HEADS // 2


def _swa_prompt_kernel(x_ref, g_ref, wqkv_ref, aq_ref, bq_ref, ak_ref, bk_ref, sink_ref,
                       wo_ref, y_ref, kc_ref, vc_ref,
                       kall_sc, vall_sc, qexp_sc, lg_sc, p_sc, out_sc, o_sc, *, tq):
    n = pl.program_id(1)
    nb = tq // WINDOW
    prev = slice(0, WINDOW)
    last = slice(nb * WINDOW, (nb + 1) * WINDOW)

    @pl.when(n == 0)
    def _():
        kall_sc[prev, :] = jnp.zeros((WINDOW, KV_WIDTH), BF16)
        vall_sc[prev, :] = jnp.zeros((WINDOW, KV_WIDTH), BF16)

    @pl.when(n > 0)
    def _():
        kall_sc[prev, :] = kall_sc[last, :]
        vall_sc[prev, :] = vall_sc[last, :]

    x = x_ref[...]
    h = _rms(x, g_ref[...]).astype(BF16)
    qkv = jnp.dot(h, wqkv_ref[...], preferred_element_type=F32)
    lane = lax.broadcasted_iota(jnp.int32, (1, LANES), 1)

    def block(col, c):
        return qkv[:, col + c * LANES:col + (c + 1) * LANES]

    k_new = jnp.concatenate(
        [_head_norm_rope(block(D_MODEL, c), block(ROT_OFFSET + D_MODEL, c), ak_ref[...], bk_ref[...],
                         lane) for c in range(KV_WIDTH // LANES)], axis=1)
    v_new = qkv[:, D_MODEL + KV_WIDTH:QKV_WIDTH]
    kall_sc[WINDOW:, :] = k_new.astype(BF16)
    vall_sc[WINDOW:, :] = v_new.astype(BF16)
    kc_ref[...] = k_new[tq - WINDOW:, :]
    vc_ref[...] = v_new[tq - WINDOW:, :]
    for c in range(D_MODEL // LANES):
        qb = _head_norm_rope(block(0, c), block(ROT_OFFSET, c), aq_ref[...], bq_ref[...], lane)
        for p in (2 * c, 2 * c + 1):
            placed = _place_query_head(qb, p, lane).astype(BF16)
            for blk in range(nb):
                qexp_sc[blk, p * WINDOW:(p + 1) * WINDOW, :] = placed[blk * WINDOW:(blk + 1) * WINDOW, :]

    tq_idx = lax.broadcasted_iota(jnp.int32, (WINDOW, 1), 0)
    key_idx = lax.broadcasted_iota(jnp.int32, (1, 2 * WINDOW), 1)
    in_window = (key_idx > tq_idx) & (key_idx <= tq_idx + WINDOW)
    rows_half = HALF_HEADS * WINDOW

    def logits(blk):
        band = slice(blk * WINDOW, (blk + 2) * WINDOW)
        for half in range(2):
            hs = slice(half * rows_half, (half + 1) * rows_half)
            lg_sc[blk % 2, hs, :] = lax.dot_general(
                qexp_sc[blk, hs, :], kall_sc[band, half * LANES:(half + 1) * LANES], _NT,
                preferred_element_type=F32)

    logits(0)
    for blk in range(nb):
        if blk + 1 < nb:
            logits(blk + 1)
        first_key = jnp.where(n * nb + blk > 0, 0, WINDOW)
        valid = in_window & (key_idx >= first_key)
        for p in range(N_HEADS):
            hr = slice(p * WINDOW, (p + 1) * WINDOW)
            p_sc[hr, :] = _softmax_with_sink(lg_sc[blk % 2, hr, :], valid, sink_ref[p])
        band = slice(blk * WINDOW, (blk + 2) * WINDOW)
        for half in range(2):
            hs = slice(half * rows_half, (half + 1) * rows_half)
            out_sc[hs, :] = jnp.dot(p_sc[hs, :], vall_sc[band, half * LANES:(half + 1) * LANES],
                                    preferred_element_type=F32)
        for c in range(D_MODEL // LANES):
            oa = out_sc[(2 * c) * WINDOW:(2 * c + 1) * WINDOW, :]
            ob = out_sc[(2 * c + 1) * WINDOW:(2 * c + 2) * WINDOW, :]
            o_sc[blk * WINDOW:(blk + 1) * WINDOW, c * LANES:(c + 1) * LANES] = _gather_head_pair(
                oa, ob, lane).astype(BF16)
    y_ref[...] = x + jnp.dot(o_sc[...], wo_ref[...], preferred_element_type=F32)


def _swa_prompt_call(x, g, w_qkv, tabs, sinks, w_o, *, name):
    b, length, _ = x.shape
    tq = SWA_TQ
    assert length % tq == 0
    row_spec = pl.BlockSpec((None, tq, D_MODEL), lambda i, t: (i, t, 0))
    tab_spec = pl.BlockSpec((tq, LANES), lambda i, t: (t, 0))
    cache_spec = pl.BlockSpec((None, WINDOW, KV_WIDTH), lambda i, t: (i, 0, 0))
    return pl.pallas_call(
        functools.partial(_swa_prompt_kernel, tq=tq),
        grid=(b, length // tq),
        in_specs=[
            row_spec,
            _const_spec((1, D_MODEL)),
            _const_spec((D_MODEL, QKV_EXT_WIDTH)),
            tab_spec,
            tab_spec,
            tab_spec,
            tab_spec,
            pl.BlockSpec(memory_space=pltpu.SMEM),
            _const_spec((D_MODEL, D_MODEL)),
        ],
        out_specs=[row_spec, cache_spec, cache_spec],
        out_shape=[
            jax.ShapeDtypeStruct((b, length, D_MODEL), F32),
            jax.ShapeDtypeStruct((b, WINDOW, KV_WIDTH), F32),
            jax.ShapeDtypeStruct((b, WINDOW, KV_WIDTH), F32),
        ],
        scratch_shapes=[
            pltpu.VMEM((tq + WINDOW, KV_WIDTH), BF16),
            pltpu.VMEM((tq + WINDOW, KV_WIDTH), BF16),
            pltpu.VMEM((tq // WINDOW, N_HEADS * WINDOW, LANES), BF16),
            pltpu.VMEM((2, N_HEADS * WINDOW, 2 * WINDOW), F32),
            pltpu.VMEM((N_HEADS * WINDOW, 2 * WINDOW), BF16),
            pltpu.VMEM((N_HEADS * WINDOW, LANES), F32),
            pltpu.VMEM((tq, D_MODEL), BF16),
        ],
        compiler_params=pltpu.CompilerParams(
            dimension_semantics=("arbitrary", "arbitrary"), vmem_limit_bytes=VMEM_LIMIT),
        name=name,
    )(x, g, w_qkv, *tabs, sinks, w_o)


def _swa_sample_kernel(x_ref, g_ref, wqkv_ref, aq_ref, bq_ref, ak_ref, bk_ref, sink_ref,
                       ck_ref, cv_ref, wo_ref, y_ref, ko_ref, vo_ref,
                       q_sc, knew_sc, vnew_sc, o_sc):
    x = x_ref[...]
    h = _rms(x, g_ref[...]).astype(BF16)
    qkv = jnp.dot(h, wqkv_ref[...], preferred_element_type=F32)
    lane = lax.broadcasted_iota(jnp.int32, (1, LANES), 1)

    def block(col, c):
        return qkv[:, col + c * LANES:col + (c + 1) * LANES]

    for c in range(D_MODEL // LANES):
        q_sc[:, c * LANES:(c + 1) * LANES] = _head_norm_rope(
            block(0, c), block(ROT_OFFSET, c), aq_ref[...], bq_ref[...], lane)
    for c in range(KV_WIDTH // LANES):
        knew_sc[:, c * LANES:(c + 1) * LANES] = _head_norm_rope(
            block(D_MODEL, c), block(ROT_OFFSET + D_MODEL, c), ak_ref[...], bk_ref[...], lane)
    vnew_sc[...] = qkv[:, D_MODEL + KV_WIDTH:QKV_WIDTH]

    t_idx = lax.broadcasted_iota(jnp.int32, (N_HEADS * DEC_SEQ, 1), 0) & (DEC_SEQ - 1)
    key_idx = lax.broadcasted_iota(jnp.int32, (1, 2 * WINDOW), 1)
    valid = ((key_idx < WINDOW) & (key_idx > t_idx)) | (
        (key_idx >= WINDOW) & (key_idx - WINDOW <= t_idx))
    sink = sink_ref[...]
    rows_half = HALF_HEADS * DEC_SEQ

    pad = jnp.zeros((WINDOW - DEC_SEQ, KV_WIDTH), F32)

    def rows(b):
        return slice(b * DEC_SEQ, (b + 1) * DEC_SEQ)

    lgs = []
    for b in range(SWA_TB):
        k_old = ck_ref[b]
        k_b = knew_sc[rows(b), :]
        ko_ref[b] = jnp.concatenate([k_old[DEC_SEQ:], k_b], axis=0)
        k_all = jnp.concatenate([k_old, k_b, pad], axis=0).astype(BF16)
        qb = q_sc[rows(b), :]
        pieces = [_place_query_head(qb[:, (p // 2) * LANES:(p // 2 + 1) * LANES], p, lane)
                  for p in range(N_HEADS)]
        halves = []
        for half in range(2):
            qe = jnp.concatenate(pieces[half * HALF_HEADS:(half + 1) * HALF_HEADS], axis=0)
            halves.append(lax.dot_general(qe.astype(BF16), k_all[:, half * LANES:(half + 1) * LANES],
                                          _NT, preferred_element_type=F32))
        lgs.append(jnp.concatenate(halves, axis=0))
    ps = [_softmax_with_sink(lg, valid, sink) for lg in lgs]
    for b in range(SWA_TB):
        v_old = cv_ref[b]
        v_b = vnew_sc[rows(b), :]
        vo_ref[b] = jnp.concatenate([v_old[DEC_SEQ:], v_b], axis=0)
        v_all = jnp.concatenate([v_old, v_b, pad], axis=0).astype(BF16)
        out = jnp.concatenate(
            [jnp.dot(ps[b][half * rows_half:(half + 1) * rows_half, :],
                     v_all[:, half * LANES:(half + 1) * LANES], preferred_element_type=F32)
             for half in range(2)], axis=0)
        o_blocks = []
        for c in range(D_MODEL // LANES):
            oa = out[(2 * c) * DEC_SEQ:(2 * c + 1) * DEC_SEQ, :]
            ob = out[(2 * c + 1) * DEC_SEQ:(2 * c + 2) * DEC_SEQ, :]
            o_blocks.append(_gather_head_pair(oa, ob, lane))
        o_sc[rows(b), :] = jnp.concatenate(o_blocks, axis=1)
    y_ref[...] = x + jnp.dot(o_sc[...].astype(BF16), wo_ref[...], preferred_element_type=F32)


def _swa_sample_call(x, g, w_qkv, tabs, sink_col, cache_k, cache_v, w_o, *, name):
    n = x.shape[0]
    tm = SWA_TB * DEC_SEQ
    row_spec = pl.BlockSpec((tm, D_MODEL), lambda i: (i, 0))
    cache_spec = pl.BlockSpec((SWA_TB, WINDOW, KV_WIDTH), lambda i: (i, 0, 0))
    return pl.pallas_call(
        _swa_sample_kernel,
        grid=(n // tm,),
        in_specs=[
            row_spec,
            _const_spec((1, D_MODEL)),
            _const_spec((D_MODEL, QKV_EXT_WIDTH)),
            _const_spec((tm, LANES)),
            _const_spec((tm, LANES)),
            _const_spec((tm, LANES)),
            _const_spec((tm, LANES)),
            _const_spec((N_HEADS * DEC_SEQ, 1)),
            cache_spec,
            cache_spec,
            _const_spec((D_MODEL, D_MODEL)),
        ],
        out_specs=[row_spec, cache_spec, cache_spec],
        out_shape=[
            jax.ShapeDtypeStruct((n, D_MODEL), F32),
            jax.ShapeDtypeStruct((DEC_BATCH, WINDOW, KV_WIDTH), F32),
            jax.ShapeDtypeStruct((DEC_BATCH, WINDOW, KV_WIDTH), F32),
        ],
        scratch_shapes=[
            pltpu.VMEM((tm, D_MODEL), F32),
            pltpu.VMEM((tm, KV_WIDTH), F32),
            pltpu.VMEM((tm, KV_WIDTH), F32),
            pltpu.VMEM((tm, D_MODEL), F32),
        ],
        compiler_params=pltpu.CompilerParams(
            dimension_semantics=("arbitrary",), vmem_limit_bytes=VMEM_LIMIT),
        name=name,
    )(x, g, w_qkv, *tabs, sink_col, cache_k, cache_v, w_o)


def _rot_cols(w):
    w4 = w.reshape(w.shape[0], -1, 2, HEAD_DIM // 2)
    return jnp.concatenate([-w4[:, :, 1:], w4[:, :, :1]], axis=2).reshape(w.shape)


def _rope_tables(pos, q_gain, k_gain):
    half = HEAD_DIM // 2
    inv_freq = jnp.power(jnp.float32(ROPE_THETA),
                         -jnp.arange(0, HEAD_DIM, 2, dtype=F32) / HEAD_DIM)
    ang = pos.astype(F32)[:, None] * inv_freq[None, :]
    reps = LANES // half
    cos = jnp.tile(jnp.cos(ang), (1, reps))
    sin = jnp.tile(jnp.sin(ang), (1, reps))

    def per_lane(v):
        return jnp.tile(v, LANES // HEAD_DIM)[None]

    def partner(v):
        return jnp.concatenate([v[half:], v[:half]])

    scale = HEAD_DIM ** -0.5 * LOG2E
    return (cos * per_lane(q_gain * scale), sin * per_lane(partner(q_gain) * scale),
            cos * per_lane(k_gain), sin * per_lane(partner(k_gain)))


def kernel(x_prompt, x_sample, cache_swa_k, cache_swa_v, state_ffn_conv, mix_norm_g, sgu_w_in,
           sgu_ln_g, sgu_ln_b, sgu_w_s, sgu_b_s, sgu_w_out, attn_w_qkv, attn_q_norm, attn_k_norm,
           attn_sinks, attn_w_o, ffn_norm_g, ffn_w_up, ffn_conv_w, ffn_conv_b, ffn_w_down):
    n_s = DEC_BATCH * DEC_SEQ

    w_s = jnp.where(jnp.tril(jnp.ones((CHUNK, CHUNK), dtype=bool))[None], sgu_w_s[0], 0.0)
    mix_p = w_s.astype(BF16)
    eye = jnp.eye(CHUNK // DEC_SEQ, dtype=F32)
    mix_s = jnp.einsum('ab,gts->gatbs', eye, w_s[:, :DEC_SEQ, :DEC_SEQ]).reshape(
        SGU_GROUPS, CHUNK, CHUNK).astype(BF16)
    bias_p = jnp.repeat(sgu_b_s[0].T, SGU_GROUP_DIM, axis=1)
    bias_s = jnp.tile(bias_p[:DEC_SEQ], (CHUNK // DEC_SEQ, 1))
    sgu_args = (mix_norm_g[0][None], sgu_w_in[0].astype(BF16), sgu_ln_g[0][None], sgu_ln_b[0][None])
    w_out = sgu_w_out[0].astype(BF16)
    (yp,) = _sgu_call(x_prompt.reshape(BATCH * SEQ, D_MODEL), *sgu_args, mix_p, bias_p, w_out,
                      emit_v=False, name="sgu_prompt")
    ys, v_s = _sgu_call(x_sample.reshape(n_s, D_MODEL), *sgu_args, mix_s, bias_s, w_out,
                        emit_v=True, name="sgu_sample")
    yp = yp.reshape(BATCH, SEQ, D_MODEL)

    conv_p, conv_s = [], []
    ffn_weights = (ffn_norm_g[:, None, :], ffn_w_up.astype(BF16), ffn_conv_w,
                   ffn_conv_b[:, None, :], ffn_w_down.astype(BF16))

    def conv_ffn(layer, yp, ys):
        yp, st_p = _ffn_call(yp, layer, *ffn_weights, name=f"ffn_prompt_{layer}")
        st_in = state_ffn_conv[layer].reshape(DEC_BATCH, (CONV_W - 1) * 2 * D_FF)
        ys, st_s = _ffn_call(ys, layer, *ffn_weights, st_in, name=f"ffn_sample_{layer}")
        conv_p.append(st_p[:, SUBLANES - (CONV_W - 1):, :])
        conv_s.append(st_s.reshape(DEC_BATCH, CONV_W - 1, 2 * D_FF))
        return yp, ys

    yp, ys = conv_ffn(0, yp, ys)

    order = jnp.array(HEAD_ORDER)
    w_q = attn_w_qkv[0][:, :D_MODEL].reshape(D_MODEL, N_HEADS, HEAD_DIM)[:, order].reshape(
        D_MODEL, D_MODEL)
    w_k = attn_w_qkv[0][:, D_MODEL:D_MODEL + KV_WIDTH]
    w_v = attn_w_qkv[0][:, D_MODEL + KV_WIDTH:]
    w_qkv = jnp.concatenate([w_q, w_k, w_v, _rot_cols(w_q), _rot_cols(w_k)], axis=1).astype(BF16)
    w_o = attn_w_o[0].reshape(N_HEADS, HEAD_DIM, D_MODEL)[order].reshape(D_MODEL, D_MODEL).astype(BF16)
    sinks = attn_sinks[0][order] * LOG2E
    g = mix_norm_g[1][None]
    tabs_p = _rope_tables(jnp.arange(SEQ, dtype=jnp.int32), attn_q_norm[0], attn_k_norm[0])
    yp, k_p, v_p = _swa_prompt_call(yp, g, w_qkv, tabs_p, sinks, w_o, name="swa_prompt")
    pos_s = PAST_LEN + (jnp.arange(SWA_TB * DEC_SEQ, dtype=jnp.int32) % DEC_SEQ)
    tabs_s = _rope_tables(pos_s, attn_q_norm[0], attn_k_norm[0])
    sink_col = jnp.repeat(sinks, DEC_SEQ)[:, None]
    ys, k_s, v_s_cache = _swa_sample_call(
        ys, g, w_qkv, tabs_s, sink_col,
        cache_swa_k[0].reshape(DEC_BATCH, WINDOW, KV_WIDTH),
        cache_swa_v[0].reshape(DEC_BATCH, WINDOW, KV_WIDTH), w_o, name="swa_sample")

    yp, ys = conv_ffn(1, yp, ys)

    cache_shape_p = (1, BATCH, WINDOW, N_KV_HEADS, HEAD_DIM)
    cache_shape_s = (1, DEC_BATCH, WINDOW, N_KV_HEADS, HEAD_DIM)
    return (yp,
            ys.reshape(DEC_BATCH, DEC_SEQ, D_MODEL),
            v_s.reshape(1, DEC_BATCH, DEC_SEQ, SGU_WIDTH),
            k_p.reshape(cache_shape_p),
            v_p.reshape(cache_shape_p),
            k_s.reshape(cache_shape_s),
            v_s_cache.reshape(cache_shape_s),
            jnp.stack(conv_p),
            jnp.stack(conv_s))
```

```python
import functools

import jax
import jax.numpy as jnp
from jax import lax
from jax.experimental import pallas as pl
from jax.experimental.pallas import tpu as pltpu

F32 = jnp.float32
BF16 = jnp.bfloat16

D_MODEL = 1024
BATCH = 8
SEQ = 2048
DEC_BATCH = 128
DEC_SEQ = 8
PAST_LEN = 8192
CHUNK = 128
SGU_WIDTH = 2 * D_MODEL
SGU_GROUPS = 8
SGU_GROUP_DIM = SGU_WIDTH // SGU_GROUPS
HEAD_DIM = 64
N_HEADS = D_MODEL // HEAD_DIM
N_KV_HEADS = 4
KV_WIDTH = N_KV_HEADS * HEAD_DIM
QKV_WIDTH = D_MODEL + 2 * KV_WIDTH
WINDOW = 128
ROPE_THETA = 10000.0
D_FF = 2816
CONV_W = 3
EPS = 1e-6
LOG2E = 1.4426950408889634

LANES = 128
SUBLANES = 8
FF_CHUNK = 256
N_FF_CHUNKS = D_FF // FF_CHUNK
VMEM_LIMIT = 56 * 1024 * 1024

SGU_TM = 512
SGU_STREAMS = 2
SGU_COLS = 512
FFN_TM = 1024
SWA_TQ = 512
SWA_TB = 16


def _const_spec(shape):
    nd = len(shape)
    return pl.BlockSpec(shape, lambda *_: (0,) * nd, pipeline_mode=pl.Buffered(1))


def _layer_spec(shape, layer):
    nd = len(shape)
    return pl.BlockSpec((None,) + shape, lambda *_: (layer,) + (0,) * nd,
                        pipeline_mode=pl.Buffered(1))


def _rms(x, g):
    ms = jnp.mean(x * x, axis=-1, keepdims=True)
    return x * lax.rsqrt(ms + EPS) * g


def _sgu_kernel(x_ref, g_ref, win_ref, lng_ref, lnb_ref, mix_ref, bias_ref, wout_ref,
                *rest, tm, emit_v):
    if emit_v:
        y_ref, v_ref, u_sc, v_sc, vb_sc, p_sc = rest
    else:
        y_ref, u_sc, v_sc, vb_sc, p_sc = rest
        v_ref = None
    cw = SGU_COLS
    n_cols = 2 * SGU_WIDTH // cw
    streams = [slice(s * tm // SGU_STREAMS, (s + 1) * tm // SGU_STREAMS)
               for s in range(SGU_STREAMS)]

    def in_proj(rs):
        h = _rms(x_ref[rs, :], g_ref[...]).astype(BF16)

        def z(c):
            return jnp.dot(h, win_ref[:, c * cw:(c + 1) * cw], preferred_element_type=F32)

        nxt = z(0)
        for c in range(n_cols):
            cur = nxt
            if c + 1 < n_cols:
                nxt = z(c + 1)
            dst = u_sc if c < n_cols // 2 else v_sc
            c0 = (c % (n_cols // 2)) * cw
            dst[rs, c0:c0 + cw] = jax.nn.gelu(cur)

    def layer_norm(rs):
        v = v_sc[rs, :]
        mu = jnp.mean(v, axis=-1, keepdims=True)
        vc = v - mu
        var = jnp.mean(vc * vc, axis=-1, keepdims=True)
        vn = vc * lax.rsqrt(var + EPS) * lng_ref[...] + lnb_ref[...]
        if emit_v:
            v_ref[rs, :] = vn
        vb_sc[rs, :] = vn.astype(BF16)

    def mix(rs):
        for c in range(rs.start // CHUNK, rs.stop // CHUNK):
            cr = slice(c * CHUNK, (c + 1) * CHUNK)
            for g in range(SGU_GROUPS):
                cs = slice(g * SGU_GROUP_DIM, (g + 1) * SGU_GROUP_DIM)
                mixed = jnp.dot(mix_ref[g], vb_sc[cr, cs], preferred_element_type=F32) + bias_ref[:, cs]
                p_sc[cr, cs] = (u_sc[cr, cs] * mixed).astype(BF16)

    def out_proj(rs):
        y_ref[rs, :] = x_ref[rs, :] + jnp.dot(p_sc[rs, :], wout_ref[...], preferred_element_type=F32)

    for rs in streams:
        in_proj(rs)
    for i, rs in enumerate(streams):
        layer_norm(rs)
        mix(rs)
        if i > 0:
            out_proj(streams[i - 1])
    out_proj(streams[-1])


def _sgu_call(x, g, w_in, ln_g, ln_b, mix, bias, w_out, *, emit_v, name):
    n = x.shape[0]
    tm = SGU_TM
    assert n % tm == 0
    row_spec = pl.BlockSpec((tm, D_MODEL), lambda i: (i, 0))
    out_shape = [jax.ShapeDtypeStruct((n, D_MODEL), F32)]
    out_specs = [row_spec]
    if emit_v:
        out_shape.append(jax.ShapeDtypeStruct((n, SGU_WIDTH), F32))
        out_specs.append(pl.BlockSpec((tm, SGU_WIDTH), lambda i: (i, 0)))
    return pl.pallas_call(
        functools.partial(_sgu_kernel, tm=tm, emit_v=emit_v),
        grid=(n // tm,),
        in_specs=[
            row_spec,
            _const_spec((1, D_MODEL)),
            _const_spec((D_MODEL, 2 * SGU_WIDTH)),
            _const_spec((1, SGU_WIDTH)),
            _const_spec((1, SGU_WIDTH)),
            _const_spec((SGU_GROUPS, CHUNK, CHUNK)),
            _const_spec((CHUNK, SGU_WIDTH)),
            _const_spec((SGU_WIDTH, D_MODEL)),
        ],
        out_specs=out_specs,
        out_shape=out_shape,
        scratch_shapes=[
            pltpu.VMEM((tm, SGU_WIDTH), F32),
            pltpu.VMEM((tm, SGU_WIDTH), F32),
            pltpu.VMEM((tm, SGU_WIDTH), BF16),
            pltpu.VMEM((tm, SGU_WIDTH), BF16),
        ],
        compiler_params=pltpu.CompilerParams(
            dimension_semantics=("arbitrary",), vmem_limit_bytes=VMEM_LIMIT),
        name=name,
    )(x, g, w_in, ln_g, ln_b, mix, bias, w_out)


def _ffn_cols(j, part):
    lo = part * D_FF + j * FF_CHUNK
    return slice(lo, lo + FF_CHUNK)


def _ffn_kernel(*refs, tm, sample):
    if sample:
        x_ref, g_ref, wup_ref, cw_ref, cb_ref, wdown_ref, sti_ref, y_ref, st_ref, p_sc = refs
    else:
        x_ref, g_ref, wup_ref, cw_ref, cb_ref, wdown_ref, y_ref, st_ref, carry_sc, p_sc = refs

        @pl.when(pl.program_id(1) == 0)
        def _():
            carry_sc[...] = jnp.zeros_like(carry_sc)

    x = x_ref[...]
    h = _rms(x, g_ref[...]).astype(BF16)

    def up(j):
        return [jnp.dot(h, wup_ref[:, _ffn_cols(j, part)], preferred_element_type=F32)
                for part in range(2)]

    def taps(cs):
        return cw_ref[0:1, cs], cw_ref[1:2, cs], cw_ref[2:3, cs], cb_ref[:, cs]

    def conv_prompt(a, cs):
        w0, w1, w2, cb = taps(cs)
        sub = lax.broadcasted_iota(jnp.int32, (SUBLANES, 1), 0)
        c0 = carry_sc[SUBLANES - 2:SUBLANES - 1, cs]
        c1 = carry_sc[SUBLANES - 1:SUBLANES, cs]
        r1 = pltpu.roll(a, 1, 0)
        r2 = pltpu.roll(a, 2, 0)
        h1 = jnp.where(sub == 0, c1, r1[:SUBLANES])
        h2 = jnp.where(sub == 0, c0, jnp.where(sub == 1, c1, r2[:SUBLANES]))
        p1 = jnp.concatenate([h1, r1[SUBLANES:]], axis=0)
        p2 = jnp.concatenate([h2, r2[SUBLANES:]], axis=0)
        carry_sc[:, cs] = a[tm - SUBLANES:, :]
        return cb + (w0 * p2 + w1 * p1 + w2 * a)

    def conv_sample(a, cs):
        w0, w1, w2, cb = [v[None] for v in taps(cs)]
        a3 = a.reshape(tm // DEC_SEQ, DEC_SEQ, FF_CHUNK)
        t = lax.broadcasted_iota(jnp.int32, (1, DEC_SEQ, 1), 1)
        s0 = jnp.broadcast_to(sti_ref[:, 0, cs][:, None, :], a3.shape)
        s1 = jnp.broadcast_to(sti_ref[:, 1, cs][:, None, :], a3.shape)
        p1 = jnp.where(t == 0, s1, pltpu.roll(a3, 1, 1))
        p2 = jnp.where(t == 0, s0, jnp.where(t == 1, s1, pltpu.roll(a3, 2, 1)))
        st_ref[:, 0, cs] = a3[:, DEC_SEQ - 2, :]
        st_ref[:, 1, cs] = a3[:, DEC_SEQ - 1, :]
        return (cb + (w0 * p2 + w1 * p1 + w2 * a3)).reshape(tm, FF_CHUNK)

    conv = conv_sample if sample else conv_prompt
    nxt = up(0)
    for j in range(N_FF_CHUNKS):
        cur = nxt
        if j + 1 < N_FF_CHUNKS:
            nxt = up(j + 1)
        gate = conv(cur[0], _ffn_cols(j, 0))
        val = conv(cur[1], _ffn_cols(j, 1))
        p_sc[:, j * FF_CHUNK:(j + 1) * FF_CHUNK] = (gate * jax.nn.sigmoid(gate) * val).astype(BF16)
    y_ref[...] = x + jnp.dot(p_sc[...], wdown_ref[...], preferred_element_type=F32)
    if not sample:
        st_ref[...] = carry_sc[...]


def _ffn_call(x, layer, g, w_up, conv_w, conv_b, w_down, state=None, *, name):
    sample = state is not None
    tm = FFN_TM
    weight_specs = [
        _layer_spec((1, D_MODEL), layer),
        _layer_spec((D_MODEL, 2 * D_FF), layer),
        _layer_spec((CONV_W, 2 * D_FF), layer),
        _layer_spec((1, 2 * D_FF), layer),
        _layer_spec((D_FF, D_MODEL), layer),
    ]
    scratch = [pltpu.VMEM((tm, D_FF), BF16)]
    if sample:
        assert DEC_SEQ == SUBLANES
        n = x.shape[0]
        assert n % tm == 0
        grid = (n // tm,)
        row_spec = pl.BlockSpec((tm, D_MODEL), lambda i: (i, 0))
        st_spec = pl.BlockSpec((tm // DEC_SEQ, CONV_W - 1, 2 * D_FF), lambda i: (i, 0, 0))
        in_specs = [row_spec] + weight_specs + [st_spec]
        st_shape = jax.ShapeDtypeStruct(state.shape, F32)
        args = (x, g, w_up, conv_w, conv_b, w_down, state)
    else:
        b, length, _ = x.shape
        assert length % tm == 0
        grid = (b, length // tm)
        row_spec = pl.BlockSpec((None, tm, D_MODEL), lambda i, t: (i, t, 0))
        st_spec = pl.BlockSpec((None, SUBLANES, 2 * D_FF), lambda i, t: (i, 0, 0))
        in_specs = [row_spec] + weight_specs
        st_shape = jax.ShapeDtypeStruct((b, SUBLANES, 2 * D_FF), F32)
        scratch = [pltpu.VMEM((SUBLANES, 2 * D_FF), F32)] + scratch
        args = (x, g, w_up, conv_w, conv_b, w_down)
    return pl.pallas_call(
        functools.partial(_ffn_kernel, tm=tm, sample=sample),
        grid=grid,
        in_specs=in_specs,
        out_specs=[row_spec, st_spec],
        out_shape=[jax.ShapeDtypeStruct(x.shape, F32), st_shape],
        scratch_shapes=scratch,
        compiler_params=pltpu.CompilerParams(
            dimension_semantics=("arbitrary",) * len(grid), vmem_limit_bytes=VMEM_LIMIT),
        name=name,
    )(*args)


HEAD_ORDER = (0, 4, 1, 5, 2, 6, 3, 7, 8, 12, 9, 13, 10, 14, 11, 15)
ROT_OFFSET = QKV_WIDTH
QKV_EXT_WIDTH = QKV_WIDTH + D_MODEL + KV_WIDTH


def _head_norm_rope(x, xr, a_tab, b_tab, lane):
    sq = x * x
    lo = jnp.sum(jnp.where(lane < HEAD_DIM, sq, 0.0), axis=-1, keepdims=True)
    hi = jnp.sum(jnp.where(lane >= HEAD_DIM, sq, 0.0), axis=-1, keepdims=True)
    ms = jnp.where(lane < HEAD_DIM, lo, hi) * (1.0 / HEAD_DIM)
    return lax.rsqrt(ms + EPS) * (x * a_tab + xr * b_tab)


def _place_query_head(qb, p, lane):
    keep = (lane >= HEAD_DIM) if p % 2 == 1 else (lane < HEAD_DIM)
    return jnp.where(keep, qb, 0.0)


def _gather_head_pair(out_a, out_b, lane):
    return jnp.where(lane < HEAD_DIM, out_a, out_b)


def _softmax_with_sink(lg, valid, sink):
    lg = jnp.where(valid, lg, -jnp.inf)
    m = jnp.maximum(jnp.max(lg, axis=-1, keepdims=True), sink)
    p = jnp.exp2(lg - m)
    den = jnp.sum(p, axis=-1, keepdims=True) + jnp.exp2(sink - m)
    return (p / den).astype(BF16)


_NT = (((1,), (1,)), ((), ()))
HALF_HEADS = N_HEADS // 2


def _swa_prompt_kernel(x_ref, g_ref, wqkv_ref, aq_ref, bq_ref, ak_ref, bk_ref, sink_ref,
                       wo_ref, y_ref, kc_ref, vc_ref,
                       kall_sc, vall_sc, qexp_sc, lg_sc, p_sc, out_sc, o_sc, *, tq):
    n = pl.program_id(1)
    nb = tq // WINDOW
    prev = slice(0, WINDOW)
    last = slice(nb * WINDOW, (nb + 1) * WINDOW)

    @pl.when(n == 0)
    def _():
        kall_sc[prev, :] = jnp.zeros((WINDOW, KV_WIDTH), BF16)
        vall_sc[prev, :] = jnp.zeros((WINDOW, KV_WIDTH), BF16)

    @pl.when(n > 0)
    def _():
        kall_sc[prev, :] = kall_sc[last, :]
        vall_sc[prev, :] = vall_sc[last, :]

    x = x_ref[...]
    h = _rms(x, g_ref[...]).astype(BF16)
    qkv = jnp.dot(h, wqkv_ref[...], preferred_element_type=F32)
    lane = lax.broadcasted_iota(jnp.int32, (1, LANES), 1)

    def block(col, c):
        return qkv[:, col + c * LANES:col + (c + 1) * LANES]

    k_new = jnp.concatenate(
        [_head_norm_rope(block(D_MODEL, c), block(ROT_OFFSET + D_MODEL, c), ak_ref[...], bk_ref[...],
                         lane) for c in range(KV_WIDTH // LANES)], axis=1)
    v_new = qkv[:, D_MODEL + KV_WIDTH:QKV_WIDTH]
    kall_sc[WINDOW:, :] = k_new.astype(BF16)
    vall_sc[WINDOW:, :] = v_new.astype(BF16)
    kc_ref[...] = k_new[tq - WINDOW:, :]
    vc_ref[...] = v_new[tq - WINDOW:, :]
    for c in range(D_MODEL // LANES):
        qb = _head_norm_rope(block(0, c), block(ROT_OFFSET, c), aq_ref[...], bq_ref[...], lane)
        for p in (2 * c, 2 * c + 1):
            placed = _place_query_head(qb, p, lane).astype(BF16)
            for blk in range(nb):
                qexp_sc[blk, p * WINDOW:(p + 1) * WINDOW, :] = placed[blk * WINDOW:(blk + 1) * WINDOW, :]

    tq_idx = lax.broadcasted_iota(jnp.int32, (WINDOW, 1), 0)
    key_idx = lax.broadcasted_iota(jnp.int32, (1, 2 * WINDOW), 1)
    in_window = (key_idx > tq_idx) & (key_idx <= tq_idx + WINDOW)
    rows_half = HALF_HEADS * WINDOW

    def logits(blk):
        band = slice(blk * WINDOW, (blk + 2) * WINDOW)
        for half in range(2):
            hs = slice(half * rows_half, (half + 1) * rows_half)
            lg_sc[blk % 2, hs, :] = lax.dot_general(
                qexp_sc[blk, hs, :], kall_sc[band, half * LANES:(half + 1) * LANES], _NT,
                preferred_element_type=F32)

    logits(0)
    for blk in range(nb):
        if blk + 1 < nb:
            logits(blk + 1)
        first_key = jnp.where(n * nb + blk > 0, 0, WINDOW)
        valid = in_window & (key_idx >= first_key)
        for p in range(N_HEADS):
            hr = slice(p * WINDOW, (p + 1) * WINDOW)
            p_sc[hr, :] = _softmax_with_sink(lg_sc[blk % 2, hr, :], valid, sink_ref[p])
        band = slice(blk * WINDOW, (blk + 2) * WINDOW)
        for half in range(2):
            hs = slice(half * rows_half, (half + 1) * rows_half)
            out_sc[hs, :] = jnp.dot(p_sc[hs, :], vall_sc[band, half * LANES:(half + 1) * LANES],
                                    preferred_element_type=F32)
        for c in range(D_MODEL // LANES):
            oa = out_sc[(2 * c) * WINDOW:(2 * c + 1) * WINDOW, :]
            ob = out_sc[(2 * c + 1) * WINDOW:(2 * c + 2) * WINDOW, :]
            o_sc[blk * WINDOW:(blk + 1) * WINDOW, c * LANES:(c + 1) * LANES] = _gather_head_pair(
                oa, ob, lane).astype(BF16)
    y_ref[...] = x + jnp.dot(o_sc[...], wo_ref[...], preferred_element_type=F32)


def _swa_prompt_call(x, g, w_qkv, tabs, sinks, w_o, *, name):
    b, length, _ = x.shape
    tq = SWA_TQ
    assert length % tq == 0
    row_spec = pl.BlockSpec((None, tq, D_MODEL), lambda i, t: (i, t, 0))
    tab_spec = pl.BlockSpec((tq, LANES), lambda i, t: (t, 0))
    cache_spec = pl.BlockSpec((None, WINDOW, KV_WIDTH), lambda i, t: (i, 0, 0))
    return pl.pallas_call(
        functools.partial(_swa_prompt_kernel, tq=tq),
        grid=(b, length // tq),
        in_specs=[
            row_spec,
            _const_spec((1, D_MODEL)),
            _const_spec((D_MODEL, QKV_EXT_WIDTH)),
            tab_spec,
            tab_spec,
            tab_spec,
            tab_spec,
            pl.BlockSpec(memory_space=pltpu.SMEM),
            _const_spec((D_MODEL, D_MODEL)),
        ],
        out_specs=[row_spec, cache_spec, cache_spec],
        out_shape=[
            jax.ShapeDtypeStruct((b, length, D_MODEL), F32),
            jax.ShapeDtypeStruct((b, WINDOW, KV_WIDTH), F32),
            jax.ShapeDtypeStruct((b, WINDOW, KV_WIDTH), F32),
        ],
        scratch_shapes=[
            pltpu.VMEM((tq + WINDOW, KV_WIDTH), BF16),
            pltpu.VMEM((tq + WINDOW, KV_WIDTH), BF16),
            pltpu.VMEM((tq // WINDOW, N_HEADS * WINDOW, LANES), BF16),
            pltpu.VMEM((2, N_HEADS * WINDOW, 2 * WINDOW), F32),
            pltpu.VMEM((N_HEADS * WINDOW, 2 * WINDOW), BF16),
            pltpu.VMEM((N_HEADS * WINDOW, LANES), F32),
            pltpu.VMEM((tq, D_MODEL), BF16),
        ],
        compiler_params=pltpu.CompilerParams(
            dimension_semantics=("arbitrary", "arbitrary"), vmem_limit_bytes=VMEM_LIMIT),
        name=name,
    )(x, g, w_qkv, *tabs, sinks, w_o)


def _swa_sample_kernel(x_ref, g_ref, wqkv_ref, aq_ref, bq_ref, ak_ref, bk_ref, sink_ref,
                       ck_ref, cv_ref, wo_ref, y_ref, ko_ref, vo_ref,
                       q_sc, knew_sc, vnew_sc, o_sc):
    x = x_ref[...]
    h = _rms(x, g_ref[...]).astype(BF16)
    qkv = jnp.dot(h, wqkv_ref[...], preferred_element_type=F32)
    lane = lax.broadcasted_iota(jnp.int32, (1, LANES), 1)

    def block(col, c):
        return qkv[:, col + c * LANES:col + (c + 1) * LANES]

    for c in range(D_MODEL // LANES):
        q_sc[:, c * LANES:(c + 1) * LANES] = _head_norm_rope(
            block(0, c), block(ROT_OFFSET, c), aq_ref[...], bq_ref[...], lane)
    for c in range(KV_WIDTH // LANES):
        knew_sc[:, c * LANES:(c + 1) * LANES] = _head_norm_rope(
            block(D_MODEL, c), block(ROT_OFFSET + D_MODEL, c), ak_ref[...], bk_ref[...], lane)
    vnew_sc[...] = qkv[:, D_MODEL + KV_WIDTH:QKV_WIDTH]

    t_idx = lax.broadcasted_iota(jnp.int32, (N_HEADS * DEC_SEQ, 1), 0) & (DEC_SEQ - 1)
    key_idx = lax.broadcasted_iota(jnp.int32, (1, 2 * WINDOW), 1)
    valid = ((key_idx < WINDOW) & (key_idx > t_idx)) | (
        (key_idx >= WINDOW) & (key_idx - WINDOW <= t_idx))
    sink = sink_ref[...]
    rows_half = HALF_HEADS * DEC_SEQ

    pad = jnp.zeros((WINDOW - DEC_SEQ, KV_WIDTH), F32)

    def rows(b):
        return slice(b * DEC_SEQ, (b + 1) * DEC_SEQ)

    lgs = []
    for b in range(SWA_TB):
        k_old = ck_ref[b]
        k_b = knew_sc[rows(b), :]
        ko_ref[b] = jnp.concatenate([k_old[DEC_SEQ:], k_b], axis=0)
        k_all = jnp.concatenate([k_old, k_b, pad], axis=0).astype(BF16)
        qb = q_sc[rows(b), :]
        pieces = [_place_query_head(qb[:, (p // 2) * LANES:(p // 2 + 1) * LANES], p, lane)
                  for p in range(N_HEADS)]
        halves = []
        for half in range(2):
            qe = jnp.concatenate(pieces[half * HALF_HEADS:(half + 1) * HALF_HEADS], axis=0)
            halves.append(lax.dot_general(qe.astype(BF16), k_all[:, half * LANES:(half + 1) * LANES],
                                          _NT, preferred_element_type=F32))
        lgs.append(jnp.concatenate(halves, axis=0))
    ps = [_softmax_with_sink(lg, valid, sink) for lg in lgs]
    for b in range(SWA_TB):
        v_old = cv_ref[b]
        v_b = vnew_sc[rows(b), :]
        vo_ref[b] = jnp.concatenate([v_old[DEC_SEQ:], v_b], axis=0)
        v_all = jnp.concatenate([v_old, v_b, pad], axis=0).astype(BF16)
        out = jnp.concatenate(
            [jnp.dot(ps[b][half * rows_half:(half + 1) * rows_half, :],
                     v_all[:, half * LANES:(half + 1) * LANES], preferred_element_type=F32)
             for half in range(2)], axis=0)
        o_blocks = []
        for c in range(D_MODEL // LANES):
            oa = out[(2 * c) * DEC_SEQ:(2 * c + 1) * DEC_SEQ, :]
            ob = out[(2 * c + 1) * DEC_SEQ:(2 * c + 2) * DEC_SEQ, :]
            o_blocks.append(_gather_head_pair(oa, ob, lane))
        o_sc[rows(b), :] = jnp.concatenate(o_blocks, axis=1)
    y_ref[...] = x + jnp.dot(o_sc[...].astype(BF16), wo_ref[...], preferred_element_type=F32)


def _swa_sample_call(x, g, w_qkv, tabs, sink_col, cache_k, cache_v, w_o, *, name):
    n = x.shape[0]
    tm = SWA_TB * DEC_SEQ
    row_spec = pl.BlockSpec((tm, D_MODEL), lambda i: (i, 0))
    cache_spec = pl.BlockSpec((SWA_TB, WINDOW, KV_WIDTH), lambda i: (i, 0, 0))
    return pl.pallas_call(
        _swa_sample_kernel,
        grid=(n // tm,),
        in_specs=[
            row_spec,
            _const_spec((1, D_MODEL)),
            _const_spec((D_MODEL, QKV_EXT_WIDTH)),
            _const_spec((tm, LANES)),
            _const_spec((tm, LANES)),
            _const_spec((tm, LANES)),
            _const_spec((tm, LANES)),
            _const_spec((N_HEADS * DEC_SEQ, 1)),
            cache_spec,
            cache_spec,
            _const_spec((D_MODEL, D_MODEL)),
        ],
        out_specs=[row_spec, cache_spec, cache_spec],
        out_shape=[
            jax.ShapeDtypeStruct((n, D_MODEL), F32),
            jax.ShapeDtypeStruct((DEC_BATCH, WINDOW, KV_WIDTH), F32),
            jax.ShapeDtypeStruct((DEC_BATCH, WINDOW, KV_WIDTH), F32),
        ],
        scratch_shapes=[
            pltpu.VMEM((tm, D_MODEL), F32),
            pltpu.VMEM((tm, KV_WIDTH), F32),
            pltpu.VMEM((tm, KV_WIDTH), F32),
            pltpu.VMEM((tm, D_MODEL), F32),
        ],
        compiler_params=pltpu.CompilerParams(
            dimension_semantics=("arbitrary",), vmem_limit_bytes=VMEM_LIMIT),
        name=name,
    )(x, g, w_qkv, *tabs, sink_col, cache_k, cache_v, w_o)


def _rot_cols(w):
    w4 = w.reshape(w.shape[0], -1, 2, HEAD_DIM // 2)
    return jnp.concatenate([-w4[:, :, 1:], w4[:, :, :1]], axis=2).reshape(w.shape)


def _rope_tables(pos, q_gain, k_gain):
    half = HEAD_DIM // 2
    inv_freq = jnp.power(jnp.float32(ROPE_THETA),
                         -jnp.arange(0, HEAD_DIM, 2, dtype=F32) / HEAD_DIM)
    ang = pos.astype(F32)[:, None] * inv_freq[None, :]
    reps = LANES // half
    cos = jnp.tile(jnp.cos(ang), (1, reps))
    sin = jnp.tile(jnp.sin(ang), (1, reps))

    def per_lane(v):
        return jnp.tile(v, LANES // HEAD_DIM)[None]

    def partner(v):
        return jnp.concatenate([v[half:], v[:half]])

    scale = HEAD_DIM ** -0.5 * LOG2E
    return (cos * per_lane(q_gain * scale), sin * per_lane(partner(q_gain) * scale),
            cos * per_lane(k_gain), sin * per_lane(partner(k_gain)))


def kernel(x_prompt, x_sample, cache_swa_k, cache_swa_v, state_ffn_conv, mix_norm_g, sgu_w_in,
           sgu_ln_g, sgu_ln_b, sgu_w_s, sgu_b_s, sgu_w_out, attn_w_qkv, attn_q_norm, attn_k_norm,
           attn_sinks, attn_w_o, ffn_norm_g, ffn_w_up, ffn_conv_w, ffn_conv_b, ffn_w_down):
    n_s = DEC_BATCH * DEC_SEQ

    w_s = jnp.where(jnp.tril(jnp.ones((CHUNK, CHUNK), dtype=bool))[None], sgu_w_s[0], 0.0)
    mix_p = w_s.astype(BF16)
    eye = jnp.eye(CHUNK // DEC_SEQ, dtype=F32)
    mix_s = jnp.einsum('ab,gts->gatbs', eye, w_s[:, :DEC_SEQ, :DEC_SEQ]).reshape(
        SGU_GROUPS, CHUNK, CHUNK).astype(BF16)
    bias_p = jnp.repeat(sgu_b_s[0].T, SGU_GROUP_DIM, axis=1)
    bias_s = jnp.tile(bias_p[:DEC_SEQ], (CHUNK // DEC_SEQ, 1))
    sgu_args = (mix_norm_g[0][None], sgu_w_in[0].astype(BF16), sgu_ln_g[0][None], sgu_ln_b[0][None])
    w_out = sgu_w_out[0].astype(BF16)
    (yp,) = _sgu_call(x_prompt.reshape(BATCH * SEQ, D_MODEL), *sgu_args, mix_p, bias_p, w_out,
                      emit_v=False, name="sgu_prompt")
    ys, v_s = _sgu_call(x_sample.reshape(n_s, D_MODEL), *sgu_args, mix_s, bias_s, w_out,
                        emit_v=True, name="sgu_sample")
    yp = yp.reshape(BATCH, SEQ, D_MODEL)

    conv_p, conv_s = [], []
    ffn_weights = (ffn_norm_g[:, None, :], ffn_w_up.astype(BF16), ffn_conv_w,
                   ffn_conv_b[:, None, :], ffn_w_down.astype(BF16))

    def conv_ffn(layer, yp, ys):
        yp, st_p = _ffn_call(yp, layer, *ffn_weights, name=f"ffn_prompt_{layer}")
        ys, st_s = _ffn_call(ys, layer, *ffn_weights, state_ffn_conv[layer],
                             name=f"ffn_sample_{layer}")
        conv_p.append(st_p[:, SUBLANES - (CONV_W - 1):, :])
        conv_s.append(st_s)
        return yp, ys

    yp, ys = conv_ffn(0, yp, ys)

    order = jnp.array(HEAD_ORDER)
    w_all = attn_w_qkv[0].astype(BF16)
    w_q = w_all[:, :D_MODEL].reshape(D_MODEL, N_HEADS, HEAD_DIM)[:, order].reshape(D_MODEL, D_MODEL)
    w_k = w_all[:, D_MODEL:D_MODEL + KV_WIDTH]
    w_v = w_all[:, D_MODEL + KV_WIDTH:]
    w_qkv = jnp.concatenate([w_q, w_k, w_v, _rot_cols(w_q), _rot_cols(w_k)], axis=1)
    w_o = attn_w_o[0].astype(BF16).reshape(N_HEADS, HEAD_DIM, D_MODEL)[order].reshape(D_MODEL, D_MODEL)
    sinks = attn_sinks[0][order] * LOG2E
    g = mix_norm_g[1][None]
    tabs_p = _rope_tables(jnp.arange(SEQ, dtype=jnp.int32), attn_q_norm[0], attn_k_norm[0])
    yp, k_p, v_p = _swa_prompt_call(yp, g, w_qkv, tabs_p, sinks, w_o, name="swa_prompt")
    pos_s = PAST_LEN + (jnp.arange(SWA_TB * DEC_SEQ, dtype=jnp.int32) % DEC_SEQ)
    tabs_s = _rope_tables(pos_s, attn_q_norm[0], attn_k_norm[0])
    sink_col = jnp.repeat(sinks, DEC_SEQ)[:, None]
    ys, k_s, v_s_cache = _swa_sample_call(
        ys, g, w_qkv, tabs_s, sink_col,
        cache_swa_k[0].reshape(DEC_BATCH, WINDOW, KV_WIDTH),
        cache_swa_v[0].reshape(DEC_BATCH, WINDOW, KV_WIDTH), w_o, name="swa_sample")

    yp, ys = conv_ffn(1, yp, ys)

    cache_shape_p = (1, BATCH, WINDOW, N_KV_HEADS, HEAD_DIM)
    cache_shape_s = (1, DEC_BATCH, WINDOW, N_KV_HEADS, HEAD_DIM)
    return (yp,
            ys.reshape(DEC_BATCH, DEC_SEQ, D_MODEL),
            v_s.reshape(1, DEC_BATCH, DEC_SEQ, SGU_WIDTH),
            k_p.reshape(cache_shape_p),
            v_p.reshape(cache_shape_p),
            k_s.reshape(cache_shape_s),
            v_s_cache.reshape(cache_shape_s),
            jnp.stack(conv_p),
            jnp.stack(conv_s))
```

```python
import functools

import jax
import jax.numpy as jnp
from jax import lax
from jax.experimental import pallas as pl
from jax.experimental.pallas import tpu as pltpu

F32 = jnp.float32
BF16 = jnp.bfloat16

D_MODEL = 1024
BATCH = 8
SEQ = 2048
DEC_BATCH = 128
DEC_SEQ = 8
PAST_LEN = 8192
CHUNK = 128
SGU_WIDTH = 2 * D_MODEL
SGU_GROUPS = 8
SGU_GROUP_DIM = SGU_WIDTH // SGU_GROUPS
HEAD_DIM = 64
N_HEADS = D_MODEL // HEAD_DIM
N_KV_HEADS = 4
KV_WIDTH = N_KV_HEADS * HEAD_DIM
QKV_WIDTH = D_MODEL + 2 * KV_WIDTH
WINDOW = 128
ROPE_THETA = 10000.0
D_FF = 2816
CONV_W = 3
EPS = 1e-6
LOG2E = 1.4426950408889634

LANES = 128
SUBLANES = 8
FF_CHUNK = 256
N_FF_CHUNKS = D_FF // FF_CHUNK
VMEM_LIMIT = 56 * 1024 * 1024

SGU_TM = 1024
SGU_STREAM_ROWS = 256
SGU_COLS = 512
FFN_TM = 1024
SWA_TQ = 1024
SWA_TB = 32


def _const_spec(shape):
    nd = len(shape)
    return pl.BlockSpec(shape, lambda *_: (0,) * nd, pipeline_mode=pl.Buffered(1))


def _layer_spec(shape, layer):
    nd = len(shape)
    return pl.BlockSpec((None,) + shape, lambda *_: (layer,) + (0,) * nd,
                        pipeline_mode=pl.Buffered(1))


def _rms(x, g):
    ms = jnp.mean(x * x, axis=-1, keepdims=True)
    return x * lax.rsqrt(ms + EPS) * g


def _sgu_kernel(x_ref, g_ref, win_ref, lng_ref, lnb_ref, mix_ref, bias_ref, wout_ref,
                *rest, tm, emit_v):
    if emit_v:
        y_ref, v_ref, u_sc, v_sc, vb_sc, p_sc = rest
    else:
        y_ref, u_sc, v_sc, vb_sc, p_sc = rest
        v_ref = None
    cw = SGU_COLS
    n_cols = 2 * SGU_WIDTH // cw
    streams = [slice(r, r + SGU_STREAM_ROWS) for r in range(0, tm, SGU_STREAM_ROWS)]

    def in_proj(rs):
        h = _rms(x_ref[rs, :], g_ref[...]).astype(BF16)

        def z(c):
            return jnp.dot(h, win_ref[:, c * cw:(c + 1) * cw], preferred_element_type=F32)

        nxt = z(0)
        for c in range(n_cols):
            cur = nxt
            if c + 1 < n_cols:
                nxt = z(c + 1)
            dst = u_sc if c < n_cols // 2 else v_sc
            c0 = (c % (n_cols // 2)) * cw
            dst[rs, c0:c0 + cw] = jax.nn.gelu(cur)

    def layer_norm(rs):
        v = v_sc[rs, :]
        mu = jnp.mean(v, axis=-1, keepdims=True)
        vc = v - mu
        var = jnp.mean(vc * vc, axis=-1, keepdims=True)
        vn = vc * lax.rsqrt(var + EPS) * lng_ref[...] + lnb_ref[...]
        if emit_v:
            v_ref[rs, :] = vn
        vb_sc[rs, :] = vn.astype(BF16)

    def mix(rs):
        for c in range(rs.start // CHUNK, rs.stop // CHUNK):
            cr = slice(c * CHUNK, (c + 1) * CHUNK)
            for g in range(SGU_GROUPS):
                cs = slice(g * SGU_GROUP_DIM, (g + 1) * SGU_GROUP_DIM)
                mixed = jnp.dot(mix_ref[g], vb_sc[cr, cs], preferred_element_type=F32) + bias_ref[:, cs]
                p_sc[cr, cs] = (u_sc[cr, cs] * mixed).astype(BF16)

    def out_proj(rs):
        y_ref[rs, :] = x_ref[rs, :] + jnp.dot(p_sc[rs, :], wout_ref[...], preferred_element_type=F32)

    for rs in streams:
        in_proj(rs)
    for i, rs in enumerate(streams):
        layer_norm(rs)
        mix(rs)
        if i > 0:
            out_proj(streams[i - 1])
    out_proj(streams[-1])


def _sgu_call(x, g, w_in, ln_g, ln_b, mix, bias, w_out, *, emit_v, name):
    n = x.shape[0]
    tm = SGU_TM // 2 if emit_v else SGU_TM
    assert n % tm == 0
    row_spec = pl.BlockSpec((tm, D_MODEL), lambda i: (i, 0))
    out_shape = [jax.ShapeDtypeStruct((n, D_MODEL), F32)]
    out_specs = [row_spec]
    if emit_v:
        out_shape.append(jax.ShapeDtypeStruct((n, SGU_WIDTH), F32))
        out_specs.append(pl.BlockSpec((tm, SGU_WIDTH), lambda i: (i, 0)))
    return pl.pallas_call(
        functools.partial(_sgu_kernel, tm=tm, emit_v=emit_v),
        grid=(n // tm,),
        in_specs=[
            row_spec,
            _const_spec((1, D_MODEL)),
            _const_spec((D_MODEL, 2 * SGU_WIDTH)),
            _const_spec((1, SGU_WIDTH)),
            _const_spec((1, SGU_WIDTH)),
            _const_spec((SGU_GROUPS, CHUNK, CHUNK)),
            _const_spec((CHUNK, SGU_WIDTH)),
            _const_spec((SGU_WIDTH, D_MODEL)),
        ],
        out_specs=out_specs,
        out_shape=out_shape,
        scratch_shapes=[
            pltpu.VMEM((tm, SGU_WIDTH), F32),
            pltpu.VMEM((tm, SGU_WIDTH), F32),
            pltpu.VMEM((tm, SGU_WIDTH), BF16),
            pltpu.VMEM((tm, SGU_WIDTH), BF16),
        ],
        compiler_params=pltpu.CompilerParams(
            dimension_semantics=("arbitrary",), vmem_limit_bytes=VMEM_LIMIT),
        name=name,
    )(x, g, w_in, ln_g, ln_b, mix, bias, w_out)


def _ffn_cols(j, part):
    lo = part * D_FF + j * FF_CHUNK
    return slice(lo, lo + FF_CHUNK)


def _ffn_kernel(*refs, tm, sample):
    if sample:
        x_ref, g_ref, wup_ref, cw_ref, cb_ref, wdown_ref, sti_ref, y_ref, st_ref, p_sc = refs
    else:
        x_ref, g_ref, wup_ref, cw_ref, cb_ref, wdown_ref, y_ref, st_ref, carry_sc, p_sc = refs

        @pl.when(pl.program_id(1) == 0)
        def _():
            carry_sc[...] = jnp.zeros_like(carry_sc)

    x = x_ref[...]
    h = _rms(x, g_ref[...]).astype(BF16)

    def up(j):
        return [jnp.dot(h, wup_ref[:, _ffn_cols(j, part)], preferred_element_type=F32)
                for part in range(2)]

    def taps(cs):
        return cw_ref[0:1, cs], cw_ref[1:2, cs], cw_ref[2:3, cs], cb_ref[:, cs]

    def conv_prompt(a, cs):
        w0, w1, w2, cb = taps(cs)
        sub = lax.broadcasted_iota(jnp.int32, (SUBLANES, 1), 0)
        c0 = carry_sc[SUBLANES - 2:SUBLANES - 1, cs]
        c1 = carry_sc[SUBLANES - 1:SUBLANES, cs]
        r1 = pltpu.roll(a, 1, 0)
        r2 = pltpu.roll(a, 2, 0)
        h1 = jnp.where(sub == 0, c1, r1[:SUBLANES])
        h2 = jnp.where(sub == 0, c0, jnp.where(sub == 1, c1, r2[:SUBLANES]))
        p1 = jnp.concatenate([h1, r1[SUBLANES:]], axis=0)
        p2 = jnp.concatenate([h2, r2[SUBLANES:]], axis=0)
        carry_sc[:, cs] = a[tm - SUBLANES:, :]
        return cb + (w0 * p2 + w1 * p1 + w2 * a)

    def conv_sample(a, cs):
        w0, w1, w2, cb = [v[None] for v in taps(cs)]
        a3 = a.reshape(tm // DEC_SEQ, DEC_SEQ, FF_CHUNK)
        t = lax.broadcasted_iota(jnp.int32, (1, DEC_SEQ, 1), 1)
        s0 = jnp.broadcast_to(sti_ref[:, 0, cs][:, None, :], a3.shape)
        s1 = jnp.broadcast_to(sti_ref[:, 1, cs][:, None, :], a3.shape)
        p1 = jnp.where(t == 0, s1, pltpu.roll(a3, 1, 1))
        p2 = jnp.where(t == 0, s0, jnp.where(t == 1, s1, pltpu.roll(a3, 2, 1)))
        st_ref[:, 0, cs] = a3[:, DEC_SEQ - 2, :]
        st_ref[:, 1, cs] = a3[:, DEC_SEQ - 1, :]
        return (cb + (w0 * p2 + w1 * p1 + w2 * a3)).reshape(tm, FF_CHUNK)

    conv = conv_sample if sample else conv_prompt
    nxt = up(0)
    for j in range(N_FF_CHUNKS):
        cur = nxt
        if j + 1 < N_FF_CHUNKS:
            nxt = up(j + 1)
        gate = conv(cur[0], _ffn_cols(j, 0))
        val = conv(cur[1], _ffn_cols(j, 1))
        p_sc[:, j * FF_CHUNK:(j + 1) * FF_CHUNK] = (gate * jax.nn.sigmoid(gate) * val).astype(BF16)
    y_ref[...] = x + jnp.dot(p_sc[...], wdown_ref[...], preferred_element_type=F32)
    if not sample:
        st_ref[...] = carry_sc[...]


def _ffn_call(x, layer, g, w_up, conv_w, conv_b, w_down, state=None, *, name):
    sample = state is not None
    tm = FFN_TM
    weight_specs = [
        _layer_spec((1, D_MODEL), layer),
        _layer_spec((D_MODEL, 2 * D_FF), layer),
        _layer_spec((CONV_W, 2 * D_FF), layer),
        _layer_spec((1, 2 * D_FF), layer),
        _layer_spec((D_FF, D_MODEL), layer),
    ]
    scratch = [pltpu.VMEM((tm, D_FF), BF16)]
    if sample:
        assert DEC_SEQ == SUBLANES
        n = x.shape[0]
        assert n % tm == 0
        grid = (n // tm,)
        row_spec = pl.BlockSpec((tm, D_MODEL), lambda i: (i, 0))
        st_spec = pl.BlockSpec((tm // DEC_SEQ, CONV_W - 1, 2 * D_FF), lambda i: (i, 0, 0))
        st_in_spec = pl.BlockSpec((None, tm // DEC_SEQ, CONV_W - 1, 2 * D_FF),
                                  lambda i: (layer, i, 0, 0))
        in_specs = [row_spec] + weight_specs + [st_in_spec]
        st_shape = jax.ShapeDtypeStruct(state.shape[1:], F32)
        args = (x, g, w_up, conv_w, conv_b, w_down, state)
    else:
        b, length, _ = x.shape
        assert length % tm == 0
        grid = (b, length // tm)
        row_spec = pl.BlockSpec((None, tm, D_MODEL), lambda i, t: (i, t, 0))
        st_spec = pl.BlockSpec((None, SUBLANES, 2 * D_FF), lambda i, t: (i, 0, 0))
        in_specs = [row_spec] + weight_specs
        st_shape = jax.ShapeDtypeStruct((b, SUBLANES, 2 * D_FF), F32)
        scratch = [pltpu.VMEM((SUBLANES, 2 * D_FF), F32)] + scratch
        args = (x, g, w_up, conv_w, conv_b, w_down)
    return pl.pallas_call(
        functools.partial(_ffn_kernel, tm=tm, sample=sample),
        grid=grid,
        in_specs=in_specs,
        out_specs=[row_spec, st_spec],
        out_shape=[jax.ShapeDtypeStruct(x.shape, F32), st_shape],
        scratch_shapes=scratch,
        compiler_params=pltpu.CompilerParams(
            dimension_semantics=("arbitrary",) * len(grid), vmem_limit_bytes=VMEM_LIMIT),
        name=name,
    )(*args)


HEAD_ORDER = (0, 4, 1, 5, 2, 6, 3, 7, 8, 12, 9, 13, 10, 14, 11, 15)
ROT_OFFSET = QKV_WIDTH
QKV_EXT_WIDTH = QKV_WIDTH + D_MODEL + KV_WIDTH


def _head_norm_rope(x, xr, a_tab, b_tab, lane):
    sq = x * x
    lo = jnp.sum(jnp.where(lane < HEAD_DIM, sq, 0.0), axis=-1, keepdims=True)
    hi = jnp.sum(jnp.where(lane >= HEAD_DIM, sq, 0.0), axis=-1, keepdims=True)
    ms = jnp.where(lane < HEAD_DIM, lo, hi) * (1.0 / HEAD_DIM)
    return lax.rsqrt(ms + EPS) * (x * a_tab + xr * b_tab)


def _place_query_head(qb, p, lane):
    keep = (lane >= HEAD_DIM) if p % 2 == 1 else (lane < HEAD_DIM)
    return jnp.where(keep, qb, 0.0)


def _gather_head_pair(out_a, out_b, lane):
    return jnp.where(lane < HEAD_DIM, out_a, out_b)


def _softmax_with_sink(lg, valid, sink):
    lg = jnp.where(valid, lg, -jnp.inf)
    m = jnp.maximum(jnp.max(lg, axis=-1, keepdims=True), sink)
    p = jnp.exp2(lg - m)
    den = jnp.sum(p, axis=-1, keepdims=True) + jnp.exp2(sink - m)
    return (p / den).astype(BF16)


_NT = (((1,), (1,)), ((), ()))
HALF_HEADS = N_HEADS // 2


def _swa_prompt_kernel(x_ref, g_ref, wqkv_ref, aq_ref, bq_ref, ak_ref, bk_ref, sink_ref,
                       wo_ref, y_ref, kc_ref, vc_ref,
                       kall_sc, vall_sc, qexp_sc, lg_sc, p_sc, out_sc, o_sc, *, tq):
    n = pl.program_id(1)
    nb = tq // WINDOW
    prev = slice(0, WINDOW)
    last = slice(nb * WINDOW, (nb + 1) * WINDOW)

    @pl.when(n == 0)
    def _():
        kall_sc[prev, :] = jnp.zeros((WINDOW, KV_WIDTH), BF16)
        vall_sc[prev, :] = jnp.zeros((WINDOW, KV_WIDTH), BF16)

    @pl.when(n > 0)
    def _():
        kall_sc[prev, :] = kall_sc[last, :]
        vall_sc[prev, :] = vall_sc[last, :]

    x = x_ref[...]
    h = _rms(x, g_ref[...]).astype(BF16)
    qkv = jnp.dot(h, wqkv_ref[...], preferred_element_type=F32)
    lane = lax.broadcasted_iota(jnp.int32, (1, LANES), 1)

    def block(col, c):
        return qkv[:, col + c * LANES:col + (c + 1) * LANES]

    k_new = jnp.concatenate(
        [_head_norm_rope(block(D_MODEL, c), block(ROT_OFFSET + D_MODEL, c), ak_ref[...], bk_ref[...],
                         lane) for c in range(KV_WIDTH // LANES)], axis=1)
    v_new = qkv[:, D_MODEL + KV_WIDTH:QKV_WIDTH]
    kall_sc[WINDOW:, :] = k_new.astype(BF16)
    vall_sc[WINDOW:, :] = v_new.astype(BF16)
    kc_ref[...] = k_new[tq - WINDOW:, :]
    vc_ref[...] = v_new[tq - WINDOW:, :]
    for c in range(D_MODEL // LANES):
        qb = _head_norm_rope(block(0, c), block(ROT_OFFSET, c), aq_ref[...], bq_ref[...], lane)
        for p in (2 * c, 2 * c + 1):
            placed = _place_query_head(qb, p, lane).astype(BF16)
            for blk in range(nb):
                qexp_sc[blk, p * WINDOW:(p + 1) * WINDOW, :] = placed[blk * WINDOW:(blk + 1) * WINDOW, :]

    tq_idx = lax.broadcasted_iota(jnp.int32, (WINDOW, 1), 0)
    key_idx = lax.broadcasted_iota(jnp.int32, (1, 2 * WINDOW), 1)
    in_window = (key_idx > tq_idx) & (key_idx <= tq_idx + WINDOW)
    rows_half = HALF_HEADS * WINDOW

    def logits(blk):
        band = slice(blk * WINDOW, (blk + 2) * WINDOW)
        for half in range(2):
            hs = slice(half * rows_half, (half + 1) * rows_half)
            lg_sc[blk % 2, hs, :] = lax.dot_general(
                qexp_sc[blk, hs, :], kall_sc[band, half * LANES:(half + 1) * LANES], _NT,
                preferred_element_type=F32)

    logits(0)
    for blk in range(nb):
        if blk + 1 < nb:
            logits(blk + 1)
        first_key = jnp.where(n * nb + blk > 0, 0, WINDOW)
        valid = in_window & (key_idx >= first_key)
        for p in range(N_HEADS):
            hr = slice(p * WINDOW, (p + 1) * WINDOW)
            p_sc[hr, :] = _softmax_with_sink(lg_sc[blk % 2, hr, :], valid, sink_ref[p])
        band = slice(blk * WINDOW, (blk + 2) * WINDOW)
        for half in range(2):
            hs = slice(half * rows_half, (half + 1) * rows_half)
            out_sc[hs, :] = jnp.dot(p_sc[hs, :], vall_sc[band, half * LANES:(half + 1) * LANES],
                                    preferred_element_type=F32)
        for c in range(D_MODEL // LANES):
            oa = out_sc[(2 * c) * WINDOW:(2 * c + 1) * WINDOW, :]
            ob = out_sc[(2 * c + 1) * WINDOW:(2 * c + 2) * WINDOW, :]
            o_sc[blk * WINDOW:(blk + 1) * WINDOW, c * LANES:(c + 1) * LANES] = _gather_head_pair(
                oa, ob, lane).astype(BF16)
    y_ref[...] = x + jnp.dot(o_sc[...], wo_ref[...], preferred_element_type=F32)


def _swa_prompt_call(x, g, w_qkv, tabs, sinks, w_o, *, name):
    b, length, _ = x.shape
    tq = SWA_TQ
    assert length % tq == 0
    row_spec = pl.BlockSpec((None, tq, D_MODEL), lambda i, t: (i, t, 0))
    tab_spec = pl.BlockSpec((tq, LANES), lambda i, t: (t, 0))
    cache_spec = pl.BlockSpec((None, WINDOW, KV_WIDTH), lambda i, t: (i, 0, 0))
    return pl.pallas_call(
        functools.partial(_swa_prompt_kernel, tq=tq),
        grid=(b, length // tq),
        in_specs=[
            row_spec,
            _const_spec((1, D_MODEL)),
            _const_spec((D_MODEL, QKV_EXT_WIDTH)),
            tab_spec,
            tab_spec,
            tab_spec,
            tab_spec,
            pl.BlockSpec(memory_space=pltpu.SMEM),
            _const_spec((D_MODEL, D_MODEL)),
        ],
        out_specs=[row_spec, cache_spec, cache_spec],
        out_shape=[
            jax.ShapeDtypeStruct((b, length, D_MODEL), F32),
            jax.ShapeDtypeStruct((b, WINDOW, KV_WIDTH), F32),
            jax.ShapeDtypeStruct((b, WINDOW, KV_WIDTH), F32),
        ],
        scratch_shapes=[
            pltpu.VMEM((tq + WINDOW, KV_WIDTH), BF16),
            pltpu.VMEM((tq + WINDOW, KV_WIDTH), BF16),
            pltpu.VMEM((tq // WINDOW, N_HEADS * WINDOW, LANES), BF16),
            pltpu.VMEM((2, N_HEADS * WINDOW, 2 * WINDOW), F32),
            pltpu.VMEM((N_HEADS * WINDOW, 2 * WINDOW), BF16),
            pltpu.VMEM((N_HEADS * WINDOW, LANES), F32),
            pltpu.VMEM((tq, D_MODEL), BF16),
        ],
        compiler_params=pltpu.CompilerParams(
            dimension_semantics=("arbitrary", "arbitrary"), vmem_limit_bytes=VMEM_LIMIT),
        name=name,
    )(x, g, w_qkv, *tabs, sinks, w_o)


def _swa_sample_kernel(x_ref, g_ref, wqkv_ref, aq_ref, bq_ref, ak_ref, bk_ref, sink_ref,
                       ck_ref, cv_ref, wo_ref, y_ref, ko_ref, vo_ref,
                       q_sc, knew_sc, vnew_sc, o_sc):
    x = x_ref[...]
    h = _rms(x, g_ref[...]).astype(BF16)
    qkv = jnp.dot(h, wqkv_ref[...], preferred_element_type=F32)
    lane = lax.broadcasted_iota(jnp.int32, (1, LANES), 1)

    def block(col, c):
        return qkv[:, col + c * LANES:col + (c + 1) * LANES]

    for c in range(D_MODEL // LANES):
        q_sc[:, c * LANES:(c + 1) * LANES] = _head_norm_rope(
            block(0, c), block(ROT_OFFSET, c), aq_ref[...], bq_ref[...], lane)
    for c in range(KV_WIDTH // LANES):
        knew_sc[:, c * LANES:(c + 1) * LANES] = _head_norm_rope(
            block(D_MODEL, c), block(ROT_OFFSET + D_MODEL, c), ak_ref[...], bk_ref[...], lane)
    vnew_sc[...] = qkv[:, D_MODEL + KV_WIDTH:QKV_WIDTH]

    t_idx = lax.broadcasted_iota(jnp.int32, (N_HEADS * DEC_SEQ, 1), 0) & (DEC_SEQ - 1)
    key_idx = lax.broadcasted_iota(jnp.int32, (1, 2 * WINDOW), 1)
    valid = ((key_idx < WINDOW) & (key_idx > t_idx)) | (
        (key_idx >= WINDOW) & (key_idx - WINDOW <= t_idx))
    sink = sink_ref[...]
    rows_half = HALF_HEADS * DEC_SEQ

    pad = jnp.zeros((WINDOW - DEC_SEQ, KV_WIDTH), F32)

    def rows(b):
        return slice(b * DEC_SEQ, (b + 1) * DEC_SEQ)

    lgs = []
    for b in range(SWA_TB):
        k_old = ck_ref[b]
        k_b = knew_sc[rows(b), :]
        ko_ref[b] = jnp.concatenate([k_old[DEC_SEQ:], k_b], axis=0)
        k_all = jnp.concatenate([k_old, k_b, pad], axis=0).astype(BF16)
        qb = q_sc[rows(b), :]
        pieces = [_place_query_head(qb[:, (p // 2) * LANES:(p // 2 + 1) * LANES], p, lane)
                  for p in range(N_HEADS)]
        halves = []
        for half in range(2):
            qe = jnp.concatenate(pieces[half * HALF_HEADS:(half + 1) * HALF_HEADS], axis=0)
            halves.append(lax.dot_general(qe.astype(BF16), k_all[:, half * LANES:(half + 1) * LANES],
                                          _NT, preferred_element_type=F32))
        lgs.append(jnp.concatenate(halves, axis=0))
    ps = [_softmax_with_sink(lg, valid, sink) for lg in lgs]
    for b in range(SWA_TB):
        v_old = cv_ref[b]
        v_b = vnew_sc[rows(b), :]
        vo_ref[b] = jnp.concatenate([v_old[DEC_SEQ:], v_b], axis=0)
        v_all = jnp.concatenate([v_old, v_b, pad], axis=0).astype(BF16)
        out = jnp.concatenate(
            [jnp.dot(ps[b][half * rows_half:(half + 1) * rows_half, :],
                     v_all[:, half * LANES:(half + 1) * LANES], preferred_element_type=F32)
             for half in range(2)], axis=0)
        o_blocks = []
        for c in range(D_MODEL // LANES):
            oa = out[(2 * c) * DEC_SEQ:(2 * c + 1) * DEC_SEQ, :]
            ob = out[(2 * c + 1) * DEC_SEQ:(2 * c + 2) * DEC_SEQ, :]
            o_blocks.append(_gather_head_pair(oa, ob, lane))
        o_sc[rows(b), :] = jnp.concatenate(o_blocks, axis=1)
    y_ref[...] = x + jnp.dot(o_sc[...].astype(BF16), wo_ref[...], preferred_element_type=F32)


def _swa_sample_call(x, g, w_qkv, tabs, sink_col, cache_k, cache_v, w_o, *, name):
    n = x.shape[0]
    tm = SWA_TB * DEC_SEQ
    row_spec = pl.BlockSpec((tm, D_MODEL), lambda i: (i, 0))
    cache_spec = pl.BlockSpec((SWA_TB, WINDOW, KV_WIDTH), lambda i: (i, 0, 0))
    return pl.pallas_call(
        _swa_sample_kernel,
        grid=(n // tm,),
        in_specs=[
            row_spec,
            _const_spec((1, D_MODEL)),
            _const_spec((D_MODEL, QKV_EXT_WIDTH)),
            _const_spec((tm, LANES)),
            _const_spec((tm, LANES)),
            _const_spec((tm, LANES)),
            _const_spec((tm, LANES)),
            _const_spec((N_HEADS * DEC_SEQ, 1)),
            cache_spec,
            cache_spec,
            _const_spec((D_MODEL, D_MODEL)),
        ],
        out_specs=[row_spec, cache_spec, cache_spec],
        out_shape=[
            jax.ShapeDtypeStruct((n, D_MODEL), F32),
            jax.ShapeDtypeStruct((DEC_BATCH, WINDOW, KV_WIDTH), F32),
            jax.ShapeDtypeStruct((DEC_BATCH, WINDOW, KV_WIDTH), F32),
        ],
        scratch_shapes=[
            pltpu.VMEM((tm, D_MODEL), F32),
            pltpu.VMEM((tm, KV_WIDTH), F32),
            pltpu.VMEM((tm, KV_WIDTH), F32),
            pltpu.VMEM((tm, D_MODEL), F32),
        ],
        compiler_params=pltpu.CompilerParams(
            dimension_semantics=("arbitrary",), vmem_limit_bytes=VMEM_LIMIT),
        name=name,
    )(x, g, w_qkv, *tabs, sink_col, cache_k, cache_v, w_o)


def _rot_cols(w):
    w4 = w.reshape(w.shape[0], -1, 2, HEAD_DIM // 2)
    return jnp.concatenate([-w4[:, :, 1:], w4[:, :, :1]], axis=2).reshape(w.shape)


def _rope_tables(pos, q_gain, k_gain):
    half = HEAD_DIM // 2
    inv_freq = jnp.power(jnp.float32(ROPE_THETA),
                         -jnp.arange(0, HEAD_DIM, 2, dtype=F32) / HEAD_DIM)
    ang = pos.astype(F32)[:, None] * inv_freq[None, :]
    reps = LANES // half
    cos = jnp.tile(jnp.cos(ang), (1, reps))
    sin = jnp.tile(jnp.sin(ang), (1, reps))

    def per_lane(v):
        return jnp.tile(v, LANES // HEAD_DIM)[None]

    def partner(v):
        return jnp.concatenate([v[half:], v[:half]])

    scale = HEAD_DIM ** -0.5 * LOG2E
    return (cos * per_lane(q_gain * scale), sin * per_lane(partner(q_gain) * scale),
            cos * per_lane(k_gain), sin * per_lane(partner(k_gain)))


def kernel(x_prompt, x_sample, cache_swa_k, cache_swa_v, state_ffn_conv, mix_norm_g, sgu_w_in,
           sgu_ln_g, sgu_ln_b, sgu_w_s, sgu_b_s, sgu_w_out, attn_w_qkv, attn_q_norm, attn_k_norm,
           attn_sinks, attn_w_o, ffn_norm_g, ffn_w_up, ffn_conv_w, ffn_conv_b, ffn_w_down):
    n_s = DEC_BATCH * DEC_SEQ

    w_s = jnp.where(jnp.tril(jnp.ones((CHUNK, CHUNK), dtype=bool))[None], sgu_w_s[0], 0.0)
    mix_p = w_s.astype(BF16)
    eye = jnp.eye(CHUNK // DEC_SEQ, dtype=F32)
    mix_s = jnp.einsum('ab,gts->gatbs', eye, w_s[:, :DEC_SEQ, :DEC_SEQ]).reshape(
        SGU_GROUPS, CHUNK, CHUNK).astype(BF16)
    bias_p = jnp.repeat(sgu_b_s[0].T, SGU_GROUP_DIM, axis=1)
    bias_s = jnp.tile(bias_p[:DEC_SEQ], (CHUNK // DEC_SEQ, 1))
    sgu_args = (mix_norm_g[0][None], sgu_w_in[0].astype(BF16), sgu_ln_g[0][None], sgu_ln_b[0][None])
    w_out = sgu_w_out[0].astype(BF16)
    (yp,) = _sgu_call(x_prompt.reshape(BATCH * SEQ, D_MODEL), *sgu_args, mix_p, bias_p, w_out,
                      emit_v=False, name="sgu_prompt")
    ys, v_s = _sgu_call(x_sample.reshape(n_s, D_MODEL), *sgu_args, mix_s, bias_s, w_out,
                        emit_v=True, name="sgu_sample")
    yp = yp.reshape(BATCH, SEQ, D_MODEL)

    conv_p, conv_s = [], []
    ffn_weights = (ffn_norm_g[:, None, :], ffn_w_up.astype(BF16), ffn_conv_w,
                   ffn_conv_b[:, None, :], ffn_w_down.astype(BF16))

    def conv_ffn(layer, yp, ys):
        yp, st_p = _ffn_call(yp, layer, *ffn_weights, name=f"ffn_prompt_{layer}")
        ys, st_s = _ffn_call(ys, layer, *ffn_weights, state_ffn_conv,
                             name=f"ffn_sample_{layer}")
        conv_p.append(st_p[:, SUBLANES - (CONV_W - 1):, :])
        conv_s.append(st_s)
        return yp, ys

    yp, ys = conv_ffn(0, yp, ys)

    order = jnp.array(HEAD_ORDER)
    w_all = attn_w_qkv[0].astype(BF16)
    w_q = w_all[:, :D_MODEL].reshape(D_MODEL, N_HEADS, HEAD_DIM)[:, order].reshape(D_MODEL, D_MODEL)
    w_k = w_all[:, D_MODEL:D_MODEL + KV_WIDTH]
    w_v = w_all[:, D_MODEL + KV_WIDTH:]
    w_qkv = jnp.concatenate([w_q, w_k, w_v, _rot_cols(w_q), _rot_cols(w_k)], axis=1)
    w_o = attn_w_o[0].astype(BF16).reshape(N_HEADS, HEAD_DIM, D_MODEL)[order].reshape(D_MODEL, D_MODEL)
    sinks = attn_sinks[0][order] * LOG2E
    g = mix_norm_g[1][None]
    tabs_p = _rope_tables(jnp.arange(SEQ, dtype=jnp.int32), attn_q_norm[0], attn_k_norm[0])
    yp, k_p, v_p = _swa_prompt_call(yp, g, w_qkv, tabs_p, sinks, w_o, name="swa_prompt")
    pos_s = PAST_LEN + (jnp.arange(SWA_TB * DEC_SEQ, dtype=jnp.int32) % DEC_SEQ)
    tabs_s = _rope_tables(pos_s, attn_q_norm[0], attn_k_norm[0])
    sink_col = jnp.repeat(sinks, DEC_SEQ)[:, None]
    ys, k_s, v_s_cache = _swa_sample_call(
        ys, g, w_qkv, tabs_s, sink_col,
        cache_swa_k[0].reshape(DEC_BATCH, WINDOW, KV_WIDTH),
        cache_swa_v[0].reshape(DEC_BATCH, WINDOW, KV_WIDTH), w_o, name="swa_sample")

    yp, ys = conv_ffn(1, yp, ys)

    cache_shape_p = (1, BATCH, WINDOW, N_KV_HEADS, HEAD_DIM)
    cache_shape_s = (1, DEC_BATCH, WINDOW, N_KV_HEADS, HEAD_DIM)
    return (yp,
            ys.reshape(DEC_BATCH, DEC_SEQ, D_MODEL),
            v_s.reshape(1, DEC_BATCH, DEC_SEQ, SGU_WIDTH),
            k_p.reshape(cache_shape_p),
            v_p.reshape(cache_shape_p),
            k_s.reshape(cache_shape_s),
            v_s_cache.reshape(cache_shape_s),
            jnp.stack(conv_p),
            jnp.stack(conv_s))
```
